```python
import math, functools
import jax, jax.numpy as jnp
from jax import lax
import numpy as np

D_MODEL = 2048
BATCH = 2
SEQ = 8192
DEPTH = 1
DEC_BATCH = 32
DEC_SEQ = 1
PAST_LEN = 16384
PAGE_SIZE = 128

A_HEADS = 8
A_HD = 64
A_ROT = A_HD // 4
ROPE_THETA = 500000.0
Q_BLOCK = 128
B_HEADS = 8
B_DK = 64
B_DV = 2 * B_DK
RET_CHUNK = 128
RET_THETA = 10000.0
C_HEADS = 4
C_HD = 256
MEM_LEN = 256
D_FF = 4 * D_MODEL
N_BRANCH = 3
EPS = 1e-6

A_QK = A_HEADS * 2 * A_HD
A_V = A_HEADS * 2 * A_HD
B_QK = B_HEADS * B_DK
B_V = B_HEADS * B_DV
C_W = C_HEADS * C_HD
IN_WIDTHS = (A_QK, A_QK, A_V, B_QK, B_QK, B_V, B_V, C_W, N_BRANCH * D_MODEL)
IN_SPLITS = tuple(int(v) for v in np.cumsum(IN_WIDTHS)[:-1])
IN_WIDTH = sum(IN_WIDTHS)

kernel_name = 'hybrid_diffattn_retention_decoder_step'


def _rmsnorm(x, g=None):
    xf = x.astype(jnp.float32)
    y = xf * lax.rsqrt(jnp.mean(xf * xf, axis=-1, keepdims=True) + EPS)
    if g is not None:
        y = y * g.astype(jnp.float32)
    return y.astype(x.dtype)


def _rope_tables(pos, n_rot, theta):
    inv = 1.0 / (theta ** (jnp.arange(0, n_rot, 2, dtype=jnp.float32) / n_rot))
    ang = pos.astype(jnp.float32)[:, None] * inv[None, :]
    return jnp.cos(ang), jnp.sin(ang)


def _apply_rope(x, cos, sin):
    half = cos.shape[-1]
    bshape = (cos.shape[0],) + (1,) * (x.ndim - 3) + (half,)
    c = cos.reshape(bshape)
    s = sin.reshape(bshape)
    xf = x[..., :2 * half].astype(jnp.float32)
    x1, x2 = xf[..., :half], xf[..., half:]
    rot = jnp.concatenate([x1 * c - x2 * s, x2 * c + x1 * s], axis=-1).astype(x.dtype)
    return jnp.concatenate([rot, x[..., 2 * half:]], axis=-1)


def _diff_attn_prompt(q, k, v, lam):
    b, t = q.shape[:2]
    nb = t // Q_BLOCK
    qb = q.reshape(b, nb, Q_BLOCK, A_HEADS, 2, A_HD).swapaxes(0, 1)
    kpos = jnp.arange(t)
    scale = A_HD ** -0.5

    def one_block(args):
        qi, i = args
        s = jnp.einsum('bqhcd,bshcd->bhcqs', qi, k).astype(jnp.float32) * scale
        qpos = i * Q_BLOCK + jnp.arange(Q_BLOCK)
        s = jnp.where(kpos[None, :] <= qpos[:, None], s, -jnp.inf)
        p = jax.nn.softmax(s, axis=-1)
        w = p[:, :, 0] - lam * p[:, :, 1]
        return jnp.einsum('bhqs,bshe->bqhe', w.astype(v.dtype), v)

    o = lax.map(one_block, (qb, jnp.arange(nb)))
    return o.swapaxes(0, 1).reshape(b, t, A_HEADS, 2 * A_HD)


def _softmax_update(carry, s, vb):
    m, l, acc = carry
    m_new = jnp.maximum(m, s.max(-1))
    corr = jnp.exp(m - m_new)
    p = jnp.exp(s - m_new[..., None])
    l = l * corr + p.sum(-1)
    acc = acc * corr[..., None] + jnp.einsum('bhcts,bshe->bhcte', p, vb.astype(jnp.float32))
    return (m_new, l, acc)


def _diff_attn_paged(q, k_new, v_new, lam, cache_k, cache_v, page_table, layer):
    db, t = q.shape[:2]
    scale = A_HD ** -0.5
    init = (jnp.full((db, A_HEADS, 2, t), -jnp.inf, jnp.float32),
            jnp.zeros((db, A_HEADS, 2, t), jnp.float32),
            jnp.zeros((db, A_HEADS, 2, t, 2 * A_HD), jnp.float32))

    def page_step(carry, phys):
        kb = cache_k[layer, phys]
        vb = cache_v[layer, phys]
        s = jnp.einsum('bthcd,bshcd->bhcts', q, kb).astype(jnp.float32) * scale
        return _softmax_update(carry, s, vb), None

    carry, _ = lax.scan(page_step, init, page_table.T)
    s = jnp.einsum('bthcd,bshcd->bhcts', q, k_new).astype(jnp.float32) * scale
    causal = jnp.arange(t)[None, :] <= jnp.arange(t)[:, None]
    s = jnp.where(causal, s, -jnp.inf)
    m, l, acc = _softmax_update(carry, s, v_new)
    o = acc[:, :, 0] / l[:, :, 0, :, None] - lam * acc[:, :, 1] / l[:, :, 1, :, None]
    return o.transpose(0, 2, 1, 3).astype(v_new.dtype)


def _retention(q, k, v, s0):
    b, t = q.shape[:2]
    c = math.gcd(t, RET_CHUNK)
    n = t // c
    log_g = jnp.log(1.0 - 2.0 ** (-5.0 - jnp.arange(B_HEADS, dtype=jnp.float32)))
    idx = jnp.arange(c, dtype=jnp.float32)
    rel = idx[:, None] - idx[None, :]
    decay_in = jnp.where(rel >= 0, jnp.exp(log_g[:, None, None] * jnp.maximum(rel, 0.0)), 0.0)
    decay_q = jnp.exp(log_g[:, None] * (idx + 1.0))
    decay_k = jnp.exp(log_g[:, None] * (c - 1.0 - idx))
    decay_c = jnp.exp(log_g * c)

    def chunks(a):
        return a.astype(jnp.float32).reshape(b, n, c, B_HEADS, a.shape[-1]).transpose(1, 0, 3, 2, 4)

    def step(S, inp):
        qc, kc, vc = inp
        attn = jnp.einsum('bhid,bhjd->bhij', qc, kc) * decay_in
        o = (jnp.einsum('bhij,bhje->bhie', attn, vc)
             + jnp.einsum('bhid,bhde->bhie', qc, S) * decay_q[None, :, :, None])
        S = S * decay_c[None, :, None, None] + jnp.einsum('bhjd,bhje->bhde', kc * decay_k[None, :, :, None], vc)
        return S, o

    S, o = lax.scan(step, s0.astype(jnp.float32), (chunks(q), chunks(k), chunks(v)))
    o = o.transpose(1, 0, 3, 2, 4).reshape(b, t, B_HEADS, B_DV)
    return o, S.astype(s0.dtype)


def _mem_attn(q, mk, mv):
    s = jnp.einsum('bthd,bmhd->bhtm', q, mk).astype(jnp.float32) * C_HD ** -0.5
    p = jax.nn.softmax(s, axis=-1)
    return jnp.einsum('bhtm,bmhd->bthd', p.astype(mv.dtype), mv)


def _mem_kv(mem, g_mem, w_mk, w_mv):
    b, m = mem.shape[:2]
    mn = _rmsnorm(mem, g_mem)
    return (mn @ w_mk).reshape(b, m, C_HEADS, C_HD), (mn @ w_mv).reshape(b, m, C_HEADS, C_HD)


def _layer(x, pos, attend, s0, mem_k, mem_v, lam_init, g_mix, w_in, lq1, lk1, lq2, lk2,
           g_subln, p_a, p_b, p_c, w_o, g_mlp, w_up, w_down):
    b, t, _ = x.shape
    f32 = jnp.float32
    xn = _rmsnorm(x, g_mix)
    z = xn @ w_in
    qa, ka, va, qb, kb, vb, gb, qc, gates = jnp.split(z, IN_SPLITS, axis=-1)
    cos_a, sin_a = _rope_tables(pos, A_ROT, ROPE_THETA)
    qa = _apply_rope(qa.reshape(b, t, A_HEADS, 2, A_HD), cos_a, sin_a)
    ka = _apply_rope(ka.reshape(b, t, A_HEADS, 2, A_HD), cos_a, sin_a)
    va = va.reshape(b, t, A_HEADS, 2 * A_HD)
    lam = (jnp.exp(jnp.sum(lq1.astype(f32) * lk1.astype(f32)))
           - jnp.exp(jnp.sum(lq2.astype(f32) * lk2.astype(f32))) + lam_init)
    oa = attend(qa, ka, va, lam)
    oa = (_rmsnorm(oa, g_subln) * (1.0 - lam_init)).reshape(b, t, A_V)
    cos_b, sin_b = _rope_tables(pos, B_DK, RET_THETA)
    qb = _apply_rope(qb.reshape(b, t, B_HEADS, B_DK), cos_b, sin_b)
    kb = _apply_rope(kb.reshape(b, t, B_HEADS, B_DK), cos_b, sin_b) * (B_DK ** -0.5)
    ob, s_new = _retention(qb, kb, vb.reshape(b, t, B_HEADS, B_DV), s0)
    ob = _rmsnorm(ob).astype(x.dtype).reshape(b, t, B_V) * jax.nn.silu(gb)
    oc = _mem_attn(qc.reshape(b, t, C_HEADS, C_HD), mem_k, mem_v).reshape(b, t, C_W)
    g_a, g_b, g_c = jnp.split(jax.nn.sigmoid(gates), N_BRANCH, axis=-1)
    merged = g_a * (oa @ p_a) + g_b * (ob @ p_b) + g_c * (oc @ p_c)
    h = x + merged @ w_o
    u = _rmsnorm(h, g_mlp) @ w_up
    h = h + jnp.square(jax.nn.relu(u)) @ w_down
    return h, ka, va, s_new


def setup_inputs(seed: int = 0) -> dict:
    key = jax.random.key(seed)
    ks = iter(jax.random.split(key, 40))

    def nrm(shape, scale=1.0):
        return jax.random.normal(next(ks), shape, jnp.float32) * scale

    def gain(shape):
        return 1.0 + 0.02 * nrm(shape)

    n_pages = PAST_LEN // PAGE_SIZE
    n_used = DEC_BATCH * n_pages
    n_pool = n_used + (n_used + 3) // 4
    page_table = jax.random.permutation(next(ks), n_pool)[:n_used].reshape(DEC_BATCH, n_pages).astype(jnp.int32)
    return {
        'x_prompt': nrm((BATCH, SEQ, D_MODEL)),
        'x_sample': nrm((DEC_BATCH, DEC_SEQ, D_MODEL)),
        'cache_diff_k': nrm((DEPTH, n_pool, PAGE_SIZE, A_HEADS, 2, A_HD)),
        'cache_diff_v': nrm((DEPTH, n_pool, PAGE_SIZE, A_HEADS, 2 * A_HD)),
        'state_ret': nrm((DEPTH, DEC_BATCH, B_HEADS, B_DK, B_DV), 0.5),
        'cache_mem_k': nrm((DEPTH, DEC_BATCH, MEM_LEN, C_HEADS, C_HD)),
        'cache_mem_v': nrm((DEPTH, DEC_BATCH, MEM_LEN, C_HEADS, C_HD)),
        'page_table': page_table,
        'mem_prompt': nrm((BATCH, MEM_LEN, D_MODEL)),
        'g_mix': gain((DEPTH, D_MODEL)),
        'w_in': nrm((DEPTH, D_MODEL, IN_WIDTH), D_MODEL ** -0.5),
        'lam_q1': nrm((DEPTH, A_HD), 0.1),
        'lam_k1': nrm((DEPTH, A_HD), 0.1),
        'lam_q2': nrm((DEPTH, A_HD), 0.1),
        'lam_k2': nrm((DEPTH, A_HD), 0.1),
        'g_subln': gain((DEPTH, 2 * A_HD)),
        'g_mem': gain((DEPTH, D_MODEL)),
        'w_mem_k': nrm((DEPTH, D_MODEL, C_W), D_MODEL ** -0.5),
        'w_mem_v': nrm((DEPTH, D_MODEL, C_W), D_MODEL ** -0.5),
        'p_a': nrm((DEPTH, A_V, D_MODEL), A_V ** -0.5),
        'p_b': nrm((DEPTH, B_V, D_MODEL), B_V ** -0.5),
        'p_c': nrm((DEPTH, C_W, D_MODEL), C_W ** -0.5),
        'w_o': nrm((DEPTH, D_MODEL, D_MODEL), D_MODEL ** -0.5),
        'g_mlp': gain((DEPTH, D_MODEL)),
        'w_up': nrm((DEPTH, D_MODEL, D_FF), D_MODEL ** -0.5),
        'w_down': nrm((DEPTH, D_FF, D_MODEL), D_FF ** -0.5),
        'g_final': gain((D_MODEL,)),
    }


def reference(x_prompt, x_sample, cache_diff_k, cache_diff_v, state_ret, cache_mem_k, cache_mem_v,
              page_table, mem_prompt, g_mix, w_in, lam_q1, lam_k1, lam_q2, lam_k2, g_subln, g_mem,
              w_mem_k, w_mem_v, p_a, p_b, p_c, w_o, g_mlp, w_up, w_down, g_final):
    pos_p = jnp.arange(x_prompt.shape[1])
    pos_s = PAST_LEN + jnp.arange(x_sample.shape[1])
    s0_p = jnp.zeros((x_prompt.shape[0], B_HEADS, B_DK, B_DV), x_prompt.dtype)
    hp, hs = x_prompt, x_sample
    kp_l, vp_l, sp_l, mkp_l, mvp_l, ks_l, vs_l, ss_l = [], [], [], [], [], [], [], []
    for l in range(DEPTH):
        lam_init = 0.8 - 0.6 * math.exp(-0.3 * l)
        wl = (g_mix[l], w_in[l], lam_q1[l], lam_k1[l], lam_q2[l], lam_k2[l], g_subln[l],
              p_a[l], p_b[l], p_c[l], w_o[l], g_mlp[l], w_up[l], w_down[l])
        mk_p, mv_p = _mem_kv(mem_prompt, g_mem[l], w_mem_k[l], w_mem_v[l])
        hp, kp, vp, sp = _layer(hp, pos_p, _diff_attn_prompt, s0_p, mk_p, mv_p, lam_init, *wl)
        paged = functools.partial(_diff_attn_paged, cache_k=cache_diff_k, cache_v=cache_diff_v,
                                  page_table=page_table, layer=l)
        hs, ks, vs, ss = _layer(hs, pos_s, paged, state_ret[l], cache_mem_k[l], cache_mem_v[l], lam_init, *wl)
        kp_l.append(kp); vp_l.append(vp); sp_l.append(sp); mkp_l.append(mk_p); mvp_l.append(mv_p)
        ks_l.append(ks); vs_l.append(vs); ss_l.append(ss)
    y_prompt = _rmsnorm(hp, g_final)
    y_sample = _rmsnorm(hs, g_final)
    return (y_prompt, y_sample, jnp.stack(kp_l), jnp.stack(vp_l), jnp.stack(sp_l), jnp.stack(mkp_l),
            jnp.stack(mvp_l), jnp.stack(ks_l), jnp.stack(vs_l), jnp.stack(ss_l))
```

```python
import functools
import math

import jax
import jax.numpy as jnp
from jax import lax
from jax.experimental import pallas as pl
from jax.experimental.pallas import tpu as pltpu

F32 = jnp.float32
BF16 = jnp.bfloat16
EPS = 1e-6
NEG_BIG = -1e30

LANES = 128
VMEM_LIMIT = 56 * 1024 * 1024

A_HEADS = 8
A_HD = 64
A_ROT = A_HD // 4
ROPE_THETA = 500000.0
B_HEADS = 8
B_DK = 64
B_DV = 128
RET_THETA = 10000.0
RET_CHUNK = 128
C_HEADS = 4
C_HD = 256
SEG = 1024
N_SEG = 7

NT_DIMS = (((1,), (1,)), ((), ()))
TN_DIMS = (((0,), (0,)), ((), ()))


def _cparams(sem):
    return pltpu.CompilerParams(dimension_semantics=sem, vmem_limit_bytes=VMEM_LIMIT)


def _resident(shape, index_map):
    return pl.BlockSpec(shape, index_map, pipeline_mode=pl.Buffered(1))


def _rope_angles(pos, n_rot, theta):
    inv = 1.0 / (theta ** (jnp.arange(0, n_rot, 2, dtype=F32) / n_rot))
    ang = pos.astype(F32)[:, None] * inv[None, :]
    return jnp.cos(ang), jnp.sin(ang)


def _rope_lane_tables(pos):
    t = pos.shape[0]
    ca, sa = _rope_angles(pos, A_ROT, ROPE_THETA)
    ha = A_ROT // 2
    c_a = jnp.concatenate([ca, ca, jnp.ones((t, A_HD - A_ROT), F32)], axis=1)
    p_a = jnp.concatenate([jnp.zeros((t, ha), F32), sa, jnp.zeros((t, A_HD - A_ROT), F32)], axis=1)
    n_a = jnp.concatenate([-sa, jnp.zeros((t, A_HD - ha), F32)], axis=1)
    cb, sb = _rope_angles(pos, B_DK, RET_THETA)
    hb = B_DK // 2
    c_b = jnp.concatenate([cb, cb], axis=1)
    p_b = jnp.concatenate([jnp.zeros((t, hb), F32), sb], axis=1)
    n_b = jnp.concatenate([-sb, jnp.zeros((t, hb), F32)], axis=1)
    return jnp.concatenate([jnp.tile(a, (1, 2)) for a in (c_a, p_a, n_a, c_b, p_b, n_b)], axis=1)


def _retention_tables(c):
    log_g = jnp.log(1.0 - 2.0 ** (-5.0 - jnp.arange(B_HEADS, dtype=F32)))
    idx = jnp.arange(c, dtype=F32)
    rel = idx[:, None] - idx[None, :]
    d_in = jnp.where(rel >= 0, jnp.exp(log_g[:, None, None] * jnp.maximum(rel, 0.0)), 0.0)
    d_q = jnp.exp(log_g[:, None] * (idx + 1.0))
    d_k = jnp.exp(log_g[:, None] * (c - 1.0 - idx))
    d_c = jnp.exp(log_g * c)
    d_q = jnp.broadcast_to(d_q[:, :, None], (B_HEADS, c, LANES))
    d_k = jnp.broadcast_to(d_k[:, :, None], (B_HEADS, c, B_DK)).reshape(B_HEADS // 2, 2, c, B_DK)
    d_k = d_k.transpose(0, 2, 1, 3).reshape(B_HEADS // 2, c, 2 * B_DK)
    d_c = jnp.broadcast_to(d_c[:, None, None], (B_HEADS, B_DK, LANES)).reshape(B_HEADS // 2, 2 * B_DK, LANES)
    return d_in, d_q, d_k, d_c


def _rope_cols(r, c, p, n, half):
    outs = []
    for h in range(r.shape[1] // LANES):
        blk = r[:, h * LANES:(h + 1) * LANES]
        outs.append(blk * c + pltpu.roll(blk, half, 1) * p + pltpu.roll(blk, LANES - half, 1) * n)
    return jnp.concatenate(outs, axis=1)


def _sigmoid(x):
    return 1.0 / (1.0 + jnp.exp(-x))


def _inproj_kernel(x_ref, g_ref, w_ref, tab_ref, qa_ref, kaf_ref, kab_ref, vaf_ref, vab_ref,
                   qkb_ref, vb_ref, gb_ref, qc_ref, gates_ref, xn_ref):
    j = pl.program_id(1)

    @pl.when(j == 0)
    def _():
        x = x_ref[...]
        xn = x * lax.rsqrt(jnp.mean(x * x, axis=-1, keepdims=True) + EPS) * g_ref[...]
        xn_ref[...] = xn.astype(BF16)

    r = jnp.dot(xn_ref[...], w_ref[...], preferred_element_type=F32)

    def rope_a(v):
        return _rope_cols(v, tab_ref[:, 0:LANES], tab_ref[:, LANES:2 * LANES],
                          tab_ref[:, 2 * LANES:3 * LANES], A_ROT // 2)

    def rope_b(v):
        return _rope_cols(v, tab_ref[:, 3 * LANES:4 * LANES], tab_ref[:, 4 * LANES:5 * LANES],
                          tab_ref[:, 5 * LANES:6 * LANES], B_DK // 2)

    @pl.when(j == 0)
    def _():
        qa_ref[...] = (rope_a(r) * (A_HD ** -0.5)).astype(BF16)

    @pl.when(j == 1)
    def _():
        ka = rope_a(r)
        kaf_ref[...] = ka
        kab_ref[...] = ka.astype(BF16)

    @pl.when(j == 2)
    def _():
        vaf_ref[...] = r
        vab_ref[...] = r.astype(BF16)

    @pl.when(j == 3)
    def _():
        rb = rope_b(r)
        half = SEG // 2
        qkb_ref[:, :half] = rb[:, :half].astype(BF16)
        qkb_ref[:, half:] = (rb[:, half:] * (B_DK ** -0.5)).astype(BF16)

    @pl.when(j == 4)
    def _():
        vb_ref[...] = r.astype(BF16)

    @pl.when(j == 5)
    def _():
        gb_ref[...] = (r * _sigmoid(r)).astype(BF16)

    @pl.when(j == 6)
    def _():
        qc_ref[...] = (r * (C_HD ** -0.5)).astype(BF16)

    @pl.when(j >= N_SEG)
    def _():
        gates_ref[...] = _sigmoid(r).astype(BF16)


def _inproj(x, g, w, tab, tm):
    n, d = x.shape
    n_col = w.shape[1] // SEG
    n_gate = n_col - N_SEG
    tab_blocks = tab.shape[0] // tm
    seg = lambda i, j: (i, 0)
    out_shape = [
        jax.ShapeDtypeStruct((n, SEG), BF16),
        jax.ShapeDtypeStruct((n, SEG), F32),
        jax.ShapeDtypeStruct((n, SEG), BF16),
        jax.ShapeDtypeStruct((n, SEG), F32),
        jax.ShapeDtypeStruct((n, SEG), BF16),
        jax.ShapeDtypeStruct((n, SEG), BF16),
        jax.ShapeDtypeStruct((n, SEG), BF16),
        jax.ShapeDtypeStruct((n, SEG), BF16),
        jax.ShapeDtypeStruct((n, SEG), BF16),
        jax.ShapeDtypeStruct((n, n_gate * SEG), BF16),
    ]
    out_specs = [pl.BlockSpec((tm, SEG), seg) for _ in range(9)]
    out_specs.append(pl.BlockSpec((tm, SEG), lambda i, j: (i, jnp.maximum(j - N_SEG, 0))))
    return pl.pallas_call(
        _inproj_kernel,
        grid=(n // tm, n_col),
        in_specs=[
            pl.BlockSpec((tm, d), lambda i, j: (i, 0)),
            pl.BlockSpec((1, d), lambda i, j: (0, 0)),
            pl.BlockSpec((d, SEG), lambda i, j: (0, j)),
            pl.BlockSpec((tm, 6 * LANES), lambda i, j: (i % tab_blocks, 0)),
        ],
        out_specs=out_specs,
        out_shape=out_shape,
        scratch_shapes=[pltpu.VMEM((tm, d), BF16)],
        compiler_params=_cparams(("arbitrary", "arbitrary")),
        name="inproj",
    )(x, g, w, tab)


def _lambda(lq1_ref, lk1_ref, lq2_ref, lk2_ref, lam_init):
    a = jnp.sum(lq1_ref[...] * lk1_ref[...], axis=-1, keepdims=True)
    b = jnp.sum(lq2_ref[...] * lk2_ref[...], axis=-1, keepdims=True)
    return jnp.exp(a) - jnp.exp(b) + lam_init


def _subln(o, g, lam_init):
    y = o * lax.rsqrt(jnp.mean(o * o, axis=-1, keepdims=True) + EPS) * g
    return y * (1.0 - lam_init)


def _flash_kernel(lq1_ref, lk1_ref, lq2_ref, lk2_ref, gs_ref, q_ref, k_ref, v_ref, o_ref, *, tq, lam_init):
    qi = pl.program_id(2)
    q = q_ref[...]
    lane = lax.broadcasted_iota(jnp.int32, q.shape, 1)
    zero = jnp.zeros_like(q)
    qs = jnp.concatenate([jnp.where(lane < A_HD, q, zero), jnp.where(lane >= A_HD, q, zero)], axis=0)

    def step(kj, carry, diagonal):
        m, l, acc = carry
        start = pl.multiple_of(kj * tq, tq)
        kb = k_ref[pl.ds(start, tq), :]
        vb = v_ref[pl.ds(start, tq), :]
        s = lax.dot_general(qs, kb, NT_DIMS, preferred_element_type=F32)
        if diagonal:
            row = lax.broadcasted_iota(jnp.int32, s.shape, 0) & (tq - 1)
            col = lax.broadcasted_iota(jnp.int32, s.shape, 1)
            s = jnp.where(col <= row, s, NEG_BIG)
        m_new = jnp.maximum(m, jnp.max(s, axis=-1, keepdims=True))
        corr = jnp.exp(m - m_new)
        p = jnp.exp(s - m_new)
        l = l * corr + jnp.sum(p, axis=-1, keepdims=True)
        acc = acc * corr + jnp.dot(p.astype(BF16), vb, preferred_element_type=F32)
        return m_new, l, acc

    init = (jnp.full((2 * tq, 1), NEG_BIG, F32), jnp.zeros((2 * tq, 1), F32),
            jnp.zeros((2 * tq, LANES), F32))
    carry = lax.fori_loop(0, qi, lambda kj, c: step(kj, c, False), init)
    m, l, acc = step(qi, carry, True)
    lam = _lambda(lq1_ref, lk1_ref, lq2_ref, lk2_ref, lam_init)
    o = acc[:tq] / l[:tq] - lam * (acc[tq:] / l[tq:])
    o_ref[...] = _subln(o, gs_ref[...], lam_init).astype(BF16)


def _flash_diff_attn(lams, g_sub, q, k, v, b, t, tq, lam_init):
    assert tq & (tq - 1) == 0 and t % tq == 0
    nq = t // tq
    n = b * t
    vec = lambda shape: pl.BlockSpec(shape, lambda bi, h, qi: (0, 0))
    return pl.pallas_call(
        functools.partial(_flash_kernel, tq=tq, lam_init=lam_init),
        grid=(b, A_HEADS, nq),
        in_specs=[vec((1, A_HD))] * 4 + [
            vec((1, LANES)),
            pl.BlockSpec((tq, LANES), lambda bi, h, qi: (bi * nq + qi, h)),
            pl.BlockSpec((t, LANES), lambda bi, h, qi: (bi, h)),
            pl.BlockSpec((t, LANES), lambda bi, h, qi: (bi, h)),
        ],
        out_specs=pl.BlockSpec((tq, LANES), lambda bi, h, qi: (bi * nq + qi, h)),
        out_shape=jax.ShapeDtypeStruct((n, A_HEADS * LANES), BF16),
        compiler_params=_cparams(("arbitrary", "arbitrary", "arbitrary")),
        name="flash_diff_attn",
    )(*lams, g_sub, q, k, v)


def _paged_kernel(pt_ref, lq1_ref, lk1_ref, lq2_ref, lk2_ref, gs_ref, q_ref, kn_ref, vn_ref, *rest,
                  pages, lam_init):
    k_refs = rest[:pages]
    v_refs = rest[pages:2 * pages]
    o_ref = rest[2 * pages]
    qd_ref, m_ref, l_ref, acc_ref = rest[2 * pages + 1:]
    j = pl.program_id(1)
    rows = 2 * A_HEADS
    width = A_HEADS * LANES

    @pl.when(j == 0)
    def _():
        q = jnp.broadcast_to(q_ref[0].astype(F32), (rows, width))
        row = lax.broadcasted_iota(jnp.int32, (rows, width), 0)
        lane = lax.broadcasted_iota(jnp.int32, (rows, width), 1)
        qd = jnp.where(lane // A_HD == row, q, 0.0)
        qd_ref[...] = qd.astype(BF16)
        s_self = jnp.sum(qd * kn_ref[0], axis=-1, keepdims=True)
        m_ref[...] = jnp.broadcast_to(s_self, (rows, LANES))
        l_ref[...] = jnp.ones((rows, LANES), F32)
        acc_ref[...] = jnp.broadcast_to(vn_ref[0], (rows, width))

    qd = qd_ref[...]
    m = m_ref[:, 0:1]
    l = l_ref[:, 0:1]
    acc = acc_ref[...]
    for r in range(pages):
        kb = k_refs[r][0, 0].astype(BF16)
        vb = v_refs[r][0, 0].astype(BF16)
        s = lax.dot_general(qd, kb, NT_DIMS, preferred_element_type=F32)
        m_new = jnp.maximum(m, jnp.max(s, axis=-1, keepdims=True))
        corr = jnp.exp(m - m_new)
        p = jnp.exp(s - m_new)
        l = l * corr + jnp.sum(p, axis=-1, keepdims=True)
        acc = acc * corr + jnp.dot(p.astype(BF16), vb, preferred_element_type=F32)
        m = m_new
    m_ref[...] = jnp.broadcast_to(m, (rows, LANES))
    l_ref[...] = jnp.broadcast_to(l, (rows, LANES))
    acc_ref[...] = acc

    @pl.when(j == pl.num_programs(1) - 1)
    def _():
        lam = _lambda(lq1_ref, lk1_ref, lq2_ref, lk2_ref, lam_init)
        g = gs_ref[...]
        outs = []
        for h in range(A_HEADS):
            a0 = acc[2 * h:2 * h + 1, h * LANES:(h + 1) * LANES] / l[2 * h:2 * h + 1]
            a1 = acc[2 * h + 1:2 * h + 2, h * LANES:(h + 1) * LANES] / l[2 * h + 1:2 * h + 2]
            outs.append(_subln(a0 - lam * a1, g, lam_init))
        o_ref[0] = jnp.concatenate(outs, axis=1).astype(BF16)


def _paged_diff_attn(page_table, lams, g_sub, q, k_new, v_new, cache_k, cache_v, pages, lam_init):
    db, n_pages = page_table.shape
    _, page, width = cache_k.shape[1:]
    assert n_pages % pages == 0
    rows = 2 * A_HEADS
    vec = lambda shape: pl.BlockSpec(shape, lambda bi, j, pt: (0, 0))
    row3 = pl.BlockSpec((1, 1, width), lambda bi, j, pt: (bi, 0, 0))

    def page_spec(r):
        return pl.BlockSpec((1, 1, page, width), lambda bi, j, pt: (0, pt[bi, j * pages + r], 0, 0))

    grid_spec = pltpu.PrefetchScalarGridSpec(
        num_scalar_prefetch=1,
        grid=(db, n_pages // pages),
        in_specs=[vec((1, A_HD))] * 4 + [vec((1, LANES)), row3, row3, row3]
        + [page_spec(r) for r in range(pages)] * 2,
        out_specs=row3,
        scratch_shapes=[pltpu.VMEM((rows, width), BF16), pltpu.VMEM((rows, LANES), F32),
                        pltpu.VMEM((rows, LANES), F32), pltpu.VMEM((rows, width), F32)],
    )
    return pl.pallas_call(
        functools.partial(_paged_kernel, pages=pages, lam_init=lam_init),
        grid_spec=grid_spec,
        out_shape=jax.ShapeDtypeStruct((db, 1, width), BF16),
        compiler_params=_cparams(("arbitrary", "arbitrary")),
        name="paged_diff_attn",
    )(page_table, *lams, g_sub, q, k_new, v_new, *([cache_k] * pages), *([cache_v] * pages))


def _head_mask(x, e):
    lane = lax.broadcasted_iota(jnp.int32, x.shape, x.ndim - 1)
    keep = (lane >= B_DK) if e else (lane < B_DK)
    return jnp.where(keep, x, jnp.zeros_like(x))


def _rms_plain(o):
    return o * lax.rsqrt(jnp.mean(o * o, axis=-1, keepdims=True) + EPS)


def _retention_kernel(qk_ref, v_ref, gs_ref, din_ref, dq_ref, dk_ref, dc_ref, o_ref, s_out_ref, s_ref):
    ci = pl.program_id(1)

    @pl.when(ci == 0)
    def _():
        s_ref[...] = jnp.zeros_like(s_ref)

    half = B_HEADS * B_DK
    for p in range(B_HEADS // 2):
        q2 = qk_ref[:, p * LANES:(p + 1) * LANES]
        k2 = qk_ref[:, half + p * LANES:half + (p + 1) * LANES]
        s_old = s_ref[p]
        s_bf = s_old.astype(BF16)
        kd = (k2.astype(F32) * dk_ref[p]).astype(BF16)
        new_rows = []
        for e in range(2):
            h = 2 * p + e
            qz = _head_mask(q2, e)
            vh = v_ref[:, h * LANES:(h + 1) * LANES]
            attn = lax.dot_general(qz, k2, NT_DIMS, preferred_element_type=F32) * din_ref[h]
            o = (jnp.dot(attn.astype(BF16), vh, preferred_element_type=F32)
                 + jnp.dot(qz, s_bf, preferred_element_type=F32) * dq_ref[h])
            u = lax.dot_general(kd, vh, TN_DIMS, preferred_element_type=F32)
            rows = slice(e * B_DK, (e + 1) * B_DK)
            new_rows.append(s_old[rows] * dc_ref[p][rows] + u[rows])
            gate = gs_ref[:, h * LANES:(h + 1) * LANES].astype(F32)
            o_ref[:, h * LANES:(h + 1) * LANES] = (_rms_plain(o) * gate).astype(BF16)
        s_ref[p] = jnp.concatenate(new_rows, axis=0)

    @pl.when(ci == pl.num_programs(1) - 1)
    def _():
        s_out_ref[0] = s_ref[...].reshape(B_HEADS, B_DK, B_DV)


def _retention_prompt(qk, v, gs, b, t):
    c = math.gcd(t, RET_CHUNK)
    nc = t // c
    n = b * t
    d_in, d_q, d_k, d_c = _retention_tables(c)
    rows = lambda bi, ci: (bi * nc + ci, 0)
    const3 = lambda bi, ci: (0, 0, 0)
    width = B_HEADS * B_DV
    return pl.pallas_call(
        _retention_kernel,
        grid=(b, nc),
        in_specs=[
            pl.BlockSpec((c, 2 * B_HEADS * B_DK), rows),
            pl.BlockSpec((c, width), rows),
            pl.BlockSpec((c, width), rows),
            pl.BlockSpec(d_in.shape, const3),
            pl.BlockSpec(d_q.shape, const3),
            pl.BlockSpec(d_k.shape, const3),
            pl.BlockSpec(d_c.shape, const3),
        ],
        out_specs=[
            pl.BlockSpec((c, width), rows),
            pl.BlockSpec((1, B_HEADS, B_DK, B_DV), lambda bi, ci: (bi, 0, 0, 0)),
        ],
        out_shape=[jax.ShapeDtypeStruct((n, width), BF16),
                   jax.ShapeDtypeStruct((b, B_HEADS, B_DK, B_DV), F32)],
        scratch_shapes=[pltpu.VMEM((B_HEADS // 2, 2 * B_DK, B_DV), F32)],
        compiler_params=_cparams(("arbitrary", "arbitrary")),
        name="retention_prompt",
    )(qk, v, gs, d_in, d_q, d_k, d_c)


def _retention_step_kernel(qk_ref, v_ref, gs_ref, s0_ref, dq_ref, dc_ref, o_ref, s_out_ref):
    half = B_HEADS * B_DK
    pad = 16
    row0 = lax.broadcasted_iota(jnp.int32, (pad, LANES), 0) == 0
    qk = qk_ref[0].astype(F32)
    v = v_ref[0].astype(F32)

    def pad_rows(x):
        return jnp.where(row0, jnp.broadcast_to(x, (pad, LANES)), 0.0).astype(BF16)

    for p in range(B_HEADS // 2):
        q2 = qk[:, p * LANES:(p + 1) * LANES]
        k2 = qk[:, half + p * LANES:half + (p + 1) * LANES]
        s_old = s0_ref[0, 2 * p:2 * p + 2].reshape(2 * B_DK, B_DV)
        s_bf = s_old.astype(BF16)
        k8 = pad_rows(k2)
        new_rows = []
        for e in range(2):
            h = 2 * p + e
            qz = _head_mask(q2, e)
            vh = v[:, h * LANES:(h + 1) * LANES]
            qk_dot = jnp.sum(qz * k2, axis=-1, keepdims=True)
            qs = jnp.dot(pad_rows(qz), s_bf, preferred_element_type=F32)[0:1]
            o = qk_dot * vh + qs * dq_ref[h][0:1]
            u = lax.dot_general(k8, pad_rows(vh), TN_DIMS, preferred_element_type=F32)
            rows = slice(e * B_DK, (e + 1) * B_DK)
            new_rows.append(s_old[rows] * dc_ref[p][rows] + u[rows])
            gate = gs_ref[0][:, h * LANES:(h + 1) * LANES].astype(F32)
            o_ref[0, :, h * LANES:(h + 1) * LANES] = (_rms_plain(o) * gate).astype(BF16)
        s_out_ref[0, 2 * p:2 * p + 2] = jnp.concatenate(new_rows, axis=0).reshape(2, B_DK, B_DV)


def _retention_sample(qk, v, gs, s0):
    db = qk.shape[0]
    _, d_q, _, d_c = _retention_tables(1)
    width = B_HEADS * B_DV
    row3 = lambda w: pl.BlockSpec((1, 1, w), lambda bi: (bi, 0, 0))
    state = pl.BlockSpec((1, B_HEADS, B_DK, B_DV), lambda bi: (bi, 0, 0, 0))
    const3 = lambda bi: (0, 0, 0)
    return pl.pallas_call(
        _retention_step_kernel,
        grid=(db,),
        in_specs=[row3(2 * B_HEADS * B_DK), row3(width), row3(width), state,
                  pl.BlockSpec(d_q.shape, const3), pl.BlockSpec(d_c.shape, const3)],
        out_specs=[row3(width), state],
        out_shape=[jax.ShapeDtypeStruct((db, 1, width), BF16),
                   jax.ShapeDtypeStruct(s0.shape, F32)],
        compiler_params=_cparams(("arbitrary",)),
        name="retention_sample",
    )(qk, v, gs, s0, d_q, d_c)


def _rms_matmul_kernel(x_ref, g_ref, w_ref, of_ref, ob_ref, xn_ref):
    @pl.when(pl.program_id(1) == 0)
    def _():
        x = x_ref[...]
        xn = x * lax.rsqrt(jnp.mean(x * x, axis=-1, keepdims=True) + EPS) * g_ref[...]
        xn_ref[...] = xn.astype(BF16)

    r = jnp.dot(xn_ref[...], w_ref[...], preferred_element_type=F32)
    of_ref[...] = r
    ob_ref[...] = r.astype(BF16)


def _rms_matmul(x, g, w, tm, tn):
    n, d = x.shape
    width = w.shape[1]
    return pl.pallas_call(
        _rms_matmul_kernel,
        grid=(n // tm, width // tn),
        in_specs=[pl.BlockSpec((tm, d), lambda i, j: (i, 0)),
                  pl.BlockSpec((1, d), lambda i, j: (0, 0)),
                  pl.BlockSpec((d, tn), lambda i, j: (0, j))],
        out_specs=[pl.BlockSpec((tm, tn), lambda i, j: (i, j))] * 2,
        out_shape=[jax.ShapeDtypeStruct((n, width), F32), jax.ShapeDtypeStruct((n, width), BF16)],
        scratch_shapes=[pltpu.VMEM((tm, d), BF16)],
        compiler_params=_cparams(("arbitrary", "arbitrary")),
        name="mem_kv",
    )(x, g, w)


def _softmax_rows(s):
    m = jnp.max(s, axis=-1, keepdims=True)
    p = jnp.exp(s - m)
    return p / jnp.sum(p, axis=-1, keepdims=True)


def _mem_attn_kernel(q_ref, mk_ref, mv_ref, o_ref):
    for h in range(C_HEADS):
        cols = slice(h * C_HD, (h + 1) * C_HD)
        s = lax.dot_general(q_ref[:, cols], mk_ref[:, cols], NT_DIMS, preferred_element_type=F32)
        p = _softmax_rows(s)
        o_ref[:, cols] = jnp.dot(p.astype(BF16), mv_ref[:, cols], preferred_element_type=F32).astype(BF16)


def _mem_attn_prompt(q, mk, mv, t, tm):
    n, width = q.shape
    mem_len = mk.shape[0] // (n // t)
    per_seq = t // tm
    return pl.pallas_call(
        _mem_attn_kernel,
        grid=(n // tm,),
        in_specs=[pl.BlockSpec((tm, width), lambda i: (i, 0)),
                  pl.BlockSpec((mem_len, width), lambda i: (i // per_seq, 0)),
                  pl.BlockSpec((mem_len, width), lambda i: (i // per_seq, 0))],
        out_specs=pl.BlockSpec((tm, width), lambda i: (i, 0)),
        out_shape=jax.ShapeDtypeStruct((n, width), BF16),
        compiler_params=_cparams(("arbitrary",)),
        name="mem_attn_prompt",
    )(q, mk, mv)


def _mem_attn_step_kernel(q_ref, mk_ref, mv_ref, o_ref):
    pad = 16
    width = C_HEADS * C_HD
    q = jnp.broadcast_to(q_ref[0].astype(F32), (pad, width))
    row = lax.broadcasted_iota(jnp.int32, (pad, width), 0)
    lane = lax.broadcasted_iota(jnp.int32, (pad, width), 1)
    qd = jnp.where(lane // C_HD == row, q, 0.0).astype(BF16)
    s = lax.dot_general(qd, mk_ref[0].astype(BF16), NT_DIMS, preferred_element_type=F32)
    p = _softmax_rows(s)
    o = jnp.dot(p.astype(BF16), mv_ref[0].astype(BF16), preferred_element_type=F32)
    o_ref[0] = jnp.concatenate([o[h:h + 1, h * C_HD:(h + 1) * C_HD] for h in range(C_HEADS)],
                               axis=1).astype(BF16)


def _mem_attn_sample(q, mk, mv):
    db, mem_len, width = mk.shape
    row3 = pl.BlockSpec((1, 1, width), lambda bi: (bi, 0, 0))
    mem = pl.BlockSpec((1, mem_len, width), lambda bi: (bi, 0, 0))
    return pl.pallas_call(
        _mem_attn_step_kernel,
        grid=(db,),
        in_specs=[row3, mem, mem],
        out_specs=row3,
        out_shape=jax.ShapeDtypeStruct((db, 1, width), BF16),
        compiler_params=_cparams(("arbitrary",)),
        name="mem_attn_sample",
    )(q, mk, mv)


def _merge_kernel(x_ref, oa_ref, ob_ref, oc_ref, gates_ref, pa_ref, pb_ref, pc_ref, wo_ref, h_ref):
    d = x_ref.shape[1]
    merged = (gates_ref[:, 0:d].astype(F32) * jnp.dot(oa_ref[...], pa_ref[...], preferred_element_type=F32)
              + gates_ref[:, d:2 * d].astype(F32) * jnp.dot(ob_ref[...], pb_ref[...], preferred_element_type=F32)
              + gates_ref[:, 2 * d:3 * d].astype(F32) * jnp.dot(oc_ref[...], pc_ref[...], preferred_element_type=F32))
    h_ref[...] = x_ref[...] + jnp.dot(merged.astype(BF16), wo_ref[...], preferred_element_type=F32)


def _merge(x, oa, ob, oc, gates, p_a, p_b, p_c, w_o, tm):
    n, d = x.shape
    rows = lambda w: pl.BlockSpec((tm, w), lambda i: (i, 0))
    whole = lambda a: _resident(a.shape, lambda i: (0, 0))
    return pl.pallas_call(
        _merge_kernel,
        grid=(n // tm,),
        in_specs=[rows(d), rows(oa.shape[1]), rows(ob.shape[1]), rows(oc.shape[1]), rows(gates.shape[1]),
                  whole(p_a), whole(p_b), whole(p_c), whole(w_o)],
        out_specs=rows(d),
        out_shape=jax.ShapeDtypeStruct((n, d), F32),
        compiler_params=_cparams(("arbitrary",)),
        name="merge",
    )(x, oa, ob, oc, gates, p_a, p_b, p_c, w_o)


def _mlp_kernel(h_ref, g_ref, wu_ref, wd_ref, gf_ref, y_ref, hn_ref, acc_ref):
    f = pl.program_id(1)

    @pl.when(f == 0)
    def _():
        h = h_ref[...]
        hn = h * lax.rsqrt(jnp.mean(h * h, axis=-1, keepdims=True) + EPS) * g_ref[...]
        hn_ref[...] = hn.astype(BF16)
        acc_ref[...] = jnp.zeros_like(acc_ref)

    u = jnp.dot(hn_ref[...], wu_ref[...], preferred_element_type=F32)
    a = jnp.square(jnp.maximum(u, 0.0)).astype(BF16)
    acc_ref[...] += jnp.dot(a, wd_ref[...], preferred_element_type=F32)

    @pl.when(f == pl.num_programs(1) - 1)
    def _():
        h2 = h_ref[...] + acc_ref[...]
        y_ref[...] = h2 * lax.rsqrt(jnp.mean(h2 * h2, axis=-1, keepdims=True) + EPS) * gf_ref[...]


def _mlp(h, g_mlp, w_up, w_down, g_final, tm, tf):
    n, d = h.shape
    d_ff = w_up.shape[1]
    return pl.pallas_call(
        _mlp_kernel,
        grid=(n // tm, d_ff // tf),
        in_specs=[pl.BlockSpec((tm, d), lambda i, f: (i, 0)),
                  pl.BlockSpec((1, d), lambda i, f: (0, 0)),
                  pl.BlockSpec((d, tf), lambda i, f: (0, f)),
                  pl.BlockSpec((tf, d), lambda i, f: (f, 0)),
                  pl.BlockSpec((1, d), lambda i, f: (0, 0))],
        out_specs=pl.BlockSpec((tm, d), lambda i, f: (i, 0)),
        out_shape=jax.ShapeDtypeStruct((n, d), F32),
        scratch_shapes=[pltpu.VMEM((tm, d), BF16), pltpu.VMEM((tm, d), F32)],
        compiler_params=_cparams(("arbitrary", "arbitrary")),
        name="mlp",
    )(h, g_mlp, w_up, w_down, g_final)


def _row_tile(n, target):
    return min(n, target)


def kernel(x_prompt, x_sample, cache_diff_k, cache_diff_v, state_ret, cache_mem_k, cache_mem_v, page_table, mem_prompt, g_mix, w_in, lam_q1, lam_k1, lam_q2, lam_k2, g_subln, g_mem, w_mem_k, w_mem_v, p_a, p_b, p_c, w_o, g_mlp, w_up, w_down, g_final):
    depth = g_mix.shape[0]
    assert depth == 1
    layer = 0
    lam_init = 0.8 - 0.6 * math.exp(-0.3 * layer)
    b, t, d = x_prompt.shape
    db, dt, _ = x_sample.shape
    assert dt == 1
    n_pool, page = cache_diff_k.shape[1:3]
    past_len = page_table.shape[1] * page
    mem_len = mem_prompt.shape[1]
    n_p, n_s = b * t, db * dt

    row = lambda a: a[layer].reshape(1, -1)
    lams = (row(lam_q1), row(lam_k1), row(lam_q2), row(lam_k2))
    g_sub = row(g_subln)
    w_in_b = w_in[layer].astype(BF16)
    w_mem_b = jnp.concatenate([w_mem_k[layer], w_mem_v[layer]], axis=1).astype(BF16)
    pa_b, pb_b, pc_b, wo_b = (a[layer].astype(BF16) for a in (p_a, p_b, p_c, w_o))
    wu_b, wd_b = w_up[layer].astype(BF16), w_down[layer].astype(BF16)
    g_final2 = g_final.reshape(1, -1)

    tm_p = _row_tile(t, 512)
    xp = x_prompt.reshape(n_p, d)
    tab_p = _rope_lane_tables(jnp.arange(t))
    qa, kaf, kab, vaf, vab, qkb, vb, gsb, qc, gates = _inproj(xp, row(g_mix), w_in_b, tab_p, tm_p)
    oa = _flash_diff_attn(lams, g_sub, qa, kab, vab, b, t, _row_tile(t, 256), lam_init)
    ob, state_p = _retention_prompt(qkb, vb, gsb, b, t)
    mem_f, mem_b = _rms_matmul(mem_prompt.reshape(b * mem_len, d), row(g_mem), w_mem_b,
                               _row_tile(b * mem_len, 256), 1024)
    c_w = C_HEADS * C_HD
    oc = _mem_attn_prompt(qc, mem_b[:, :c_w], mem_b[:, c_w:], t, tm_p)
    hp = _merge(xp, oa, ob, oc, gates, pa_b, pb_b, pc_b, wo_b, _row_tile(t, 256))
    y_p = _mlp(hp, row(g_mlp), wu_b, wd_b, g_final2, tm_p, 1024)

    xs = x_sample.reshape(n_s, d)
    tab_s = jnp.tile(_rope_lane_tables(past_len + jnp.arange(dt)), (db, 1))
    qa_s, kaf_s, _, vaf_s, _, qkb_s, vb_s, gsb_s, qc_s, gates_s = _inproj(xs, row(g_mix), w_in_b, tab_s, n_s)
    width_a = A_HEADS * LANES
    r3 = lambda a: a.reshape(db, 1, a.shape[-1])
    ck = cache_diff_k[layer].reshape(1, n_pool, page, width_a)
    cv = cache_diff_v[layer].reshape(1, n_pool, page, width_a)
    oa_s = _paged_diff_attn(page_table, lams, g_sub, r3(qa_s), r3(kaf_s), r3(vaf_s), ck, cv, 4, lam_init)
    ob_s, state_s = _retention_sample(r3(qkb_s), r3(vb_s), r3(gsb_s), state_ret[layer])
    oc_s = _mem_attn_sample(r3(qc_s), cache_mem_k[layer].reshape(db, mem_len, c_w),
                            cache_mem_v[layer].reshape(db, mem_len, c_w))
    hs = _merge(xs, oa_s.reshape(n_s, -1), ob_s.reshape(n_s, -1), oc_s.reshape(n_s, -1), gates_s,
                pa_b, pb_b, pc_b, wo_b, n_s)
    y_s = _mlp(hs, row(g_mlp), wu_b, wd_b, g_final2, n_s, 1024)

    return (
        y_p.reshape(b, t, d),
        y_s.reshape(db, dt, d),
        kaf.reshape(1, b, t, A_HEADS, 2, A_HD),
        vaf.reshape(1, b, t, A_HEADS, 2 * A_HD),
        state_p[None],
        mem_f[:, :c_w].reshape(1, b, mem_len, C_HEADS, C_HD),
        mem_f[:, c_w:].reshape(1, b, mem_len, C_HEADS, C_HD),
        kaf_s.reshape(1, db, dt, A_HEADS, 2, A_HD),
        vaf_s.reshape(1, db, dt, A_HEADS, 2 * A_HD),
        state_s[None],
    )
```

```python
import functools
import math

import jax
import jax.numpy as jnp
from jax import lax
from jax.experimental import pallas as pl
from jax.experimental.pallas import tpu as pltpu

F32 = jnp.float32
BF16 = jnp.bfloat16
EPS = 1e-6
NEG_BIG = -1e30
LOG2E = math.log2(math.e)

LANES = 128
VMEM_LIMIT = 56 * 1024 * 1024

A_HEADS = 8
A_HD = 64
A_ROT = A_HD // 4
ROPE_THETA = 500000.0
B_HEADS = 8
B_DK = 64
B_DV = 128
RET_THETA = 10000.0
RET_CHUNK = 128
C_HEADS = 4
C_HD = 256
SEG = 1024
N_SEG = 7

NT_DIMS = (((1,), (1,)), ((), ()))
TN_DIMS = (((0,), (0,)), ((), ()))


def _cparams(sem):
    return pltpu.CompilerParams(dimension_semantics=sem, vmem_limit_bytes=VMEM_LIMIT)


def _resident(shape, index_map):
    return pl.BlockSpec(shape, index_map, pipeline_mode=pl.Buffered(1))


def _rope_angles(pos, n_rot, theta):
    inv = 1.0 / (theta ** (jnp.arange(0, n_rot, 2, dtype=F32) / n_rot))
    ang = pos.astype(F32)[:, None] * inv[None, :]
    return jnp.cos(ang), jnp.sin(ang)


def _rope_lane_tables(pos):
    t = pos.shape[0]
    ca, sa = _rope_angles(pos, A_ROT, ROPE_THETA)
    ha = A_ROT // 2
    c_a = jnp.concatenate([ca, ca, jnp.ones((t, A_HD - A_ROT), F32)], axis=1)
    p_a = jnp.concatenate([jnp.zeros((t, ha), F32), sa, jnp.zeros((t, A_HD - A_ROT), F32)], axis=1)
    n_a = jnp.concatenate([-sa, jnp.zeros((t, A_HD - ha), F32)], axis=1)
    cb, sb = _rope_angles(pos, B_DK, RET_THETA)
    hb = B_DK // 2
    c_b = jnp.concatenate([cb, cb], axis=1)
    p_b = jnp.concatenate([jnp.zeros((t, hb), F32), sb], axis=1)
    n_b = jnp.concatenate([-sb, jnp.zeros((t, hb), F32)], axis=1)
    return jnp.concatenate([jnp.tile(a, (1, 2)) for a in (c_a, p_a, n_a, c_b, p_b, n_b)], axis=1)


def _retention_tables(c):
    log_g = jnp.log(1.0 - 2.0 ** (-5.0 - jnp.arange(B_HEADS, dtype=F32)))
    idx = jnp.arange(c, dtype=F32)
    rel = idx[:, None] - idx[None, :]
    d_in = jnp.where(rel >= 0, jnp.exp(log_g[:, None, None] * jnp.maximum(rel, 0.0)), 0.0)
    d_q = jnp.exp(log_g[:, None] * (idx + 1.0))
    d_k = jnp.exp(log_g[:, None] * (c - 1.0 - idx))
    d_c = jnp.exp(log_g * c)
    d_q = jnp.broadcast_to(d_q[:, :, None], (B_HEADS, c, LANES))
    d_k = jnp.broadcast_to(d_k[:, :, None], (B_HEADS, c, B_DK)).reshape(B_HEADS // 2, 2, c, B_DK)
    d_k = d_k.transpose(0, 2, 1, 3).reshape(B_HEADS // 2, c, 2 * B_DK)
    d_c = jnp.broadcast_to(d_c[:, None, None], (B_HEADS, B_DK, LANES)).reshape(B_HEADS // 2, 2 * B_DK, LANES)
    return d_in, d_q, d_k, d_c


def _rope_cols(r, c, p, n, half):
    outs = []
    for h in range(r.shape[1] // LANES):
        blk = r[:, h * LANES:(h + 1) * LANES]
        outs.append(blk * c + pltpu.roll(blk, half, 1) * p + pltpu.roll(blk, LANES - half, 1) * n)
    return jnp.concatenate(outs, axis=1)


def _sigmoid(x):
    return 1.0 / (1.0 + jnp.exp(-x))


def _inproj_kernel(x_ref, g_ref, w_ref, tab_ref, *refs, token_minor):
    xn_ref = refs[-1]
    if token_minor:
        qa_ref, kaf_ref, kab_ref, vaf_ref, vtb_ref, qkb_ref, vb_ref, gb_ref, qc_ref, gates_ref = refs[:-1]
    else:
        qa_ref, kaf_ref, vaf_ref, qkb_ref, vb_ref, gb_ref, qc_ref, gates_ref = refs[:-1]
    j = pl.program_id(1)

    @pl.when(j == 0)
    def _():
        x = x_ref[...]
        xn = x * lax.rsqrt(jnp.mean(x * x, axis=-1, keepdims=True) + EPS) * g_ref[...]
        xn_ref[...] = xn.astype(BF16)

    r = jnp.dot(xn_ref[...], w_ref[...], preferred_element_type=F32)

    def rope_a(v):
        return _rope_cols(v, tab_ref[:, 0:LANES], tab_ref[:, LANES:2 * LANES],
                          tab_ref[:, 2 * LANES:3 * LANES], A_ROT // 2)

    def rope_b(v):
        return _rope_cols(v, tab_ref[:, 3 * LANES:4 * LANES], tab_ref[:, 4 * LANES:5 * LANES],
                          tab_ref[:, 5 * LANES:6 * LANES], B_DK // 2)

    @pl.when(j == 0)
    def _():
        qa = rope_a(r) * (A_HD ** -0.5 * LOG2E)
        if token_minor:
            qa_ref[0] = qa.T.astype(BF16)
        else:
            qa_ref[...] = qa.astype(BF16)

    @pl.when(j == 1)
    def _():
        ka = rope_a(r)
        if token_minor:
            kaf_ref[0] = ka.T
            kab_ref[...] = ka.astype(BF16)
        else:
            kaf_ref[...] = ka

    @pl.when(j == 2)
    def _():
        vaf_ref[...] = r
        if token_minor:
            vtb_ref[0] = r.T.astype(BF16)

    @pl.when(j == 3)
    def _():
        rb = rope_b(r)
        half = SEG // 2
        qkb_ref[:, :half] = rb[:, :half].astype(BF16)
        qkb_ref[:, half:] = (rb[:, half:] * (B_DK ** -0.5)).astype(BF16)

    @pl.when(j == 4)
    def _():
        vb_ref[...] = r.astype(BF16)

    @pl.when(j == 5)
    def _():
        gb_ref[...] = (r * _sigmoid(r)).astype(BF16)

    @pl.when(j == 6)
    def _():
        qc_ref[...] = (r * (C_HD ** -0.5)).astype(BF16)

    @pl.when(j >= N_SEG)
    def _():
        gates_ref[...] = _sigmoid(r).astype(BF16)


def _inproj(x, g, w, tab, tm, seq_len=None):
    n, d = x.shape
    n_col = w.shape[1] // SEG
    n_gate = n_col - N_SEG
    tab_blocks = tab.shape[0] // tm
    token_minor = seq_len is not None
    rows = lambda dt: (jax.ShapeDtypeStruct((n, SEG), dt), pl.BlockSpec((tm, SEG), lambda i, j: (i, 0)))
    if token_minor:
        per_seq = seq_len // tm
        tmin = lambda dt: (jax.ShapeDtypeStruct((n // seq_len, SEG, seq_len), dt),
                           pl.BlockSpec((1, SEG, tm), lambda i, j: (i // per_seq, 0, i % per_seq)))
        outs = [tmin(BF16), tmin(F32), rows(BF16), rows(F32), tmin(BF16)]
    else:
        outs = [rows(BF16), rows(F32), rows(F32)]
    outs += [rows(BF16)] * 4
    outs.append((jax.ShapeDtypeStruct((n, n_gate * SEG), BF16),
                 pl.BlockSpec((tm, SEG), lambda i, j: (i, jnp.maximum(j - N_SEG, 0)))))
    out_shape = [o[0] for o in outs]
    out_specs = [o[1] for o in outs]
    return pl.pallas_call(
        functools.partial(_inproj_kernel, token_minor=token_minor),
        grid=(n // tm, n_col),
        in_specs=[
            pl.BlockSpec((tm, d), lambda i, j: (i, 0)),
            pl.BlockSpec((1, d), lambda i, j: (0, 0)),
            pl.BlockSpec((d, SEG), lambda i, j: (0, j)),
            pl.BlockSpec((tm, 6 * LANES), lambda i, j: (i % tab_blocks, 0)),
        ],
        out_specs=out_specs,
        out_shape=out_shape,
        scratch_shapes=[pltpu.VMEM((tm, d), BF16)],
        compiler_params=_cparams(("arbitrary", "arbitrary")),
        name="inproj",
    )(x, g, w, tab)


def _lambda(lq1_ref, lk1_ref, lq2_ref, lk2_ref, lam_init):
    a = jnp.sum(lq1_ref[...] * lk1_ref[...], axis=-1, keepdims=True)
    b = jnp.sum(lq2_ref[...] * lk2_ref[...], axis=-1, keepdims=True)
    return jnp.exp(a) - jnp.exp(b) + lam_init


def _subln(o, g, lam_init):
    y = o * lax.rsqrt(jnp.mean(o * o, axis=-1, keepdims=True) + EPS) * g
    return y * (1.0 - lam_init)


def _flash_kernel(lq1_ref, lk1_ref, lq2_ref, lk2_ref, gs_ref, qt_ref, k_ref, vt_ref, o_ref, *, tq, lam_init):
    qi = pl.program_id(2)
    qt = qt_ref[0]
    feat = lax.broadcasted_iota(jnp.int32, qt.shape, 0)
    zero = jnp.zeros_like(qt)
    qs = jnp.concatenate([jnp.where(feat < A_HD, qt, zero), jnp.where(feat >= A_HD, qt, zero)], axis=1)

    def scores(kj):
        start = pl.multiple_of(kj * tq, tq)
        return jnp.dot(k_ref[pl.ds(start, tq), :], qs, preferred_element_type=F32)

    def update(kj, s, m, l, acc):
        start = pl.multiple_of(kj * tq, tq)
        vb = vt_ref[0, :, pl.ds(start, tq)]
        m_new = jnp.maximum(m, jnp.max(s, axis=0, keepdims=True))
        corr = jnp.exp2(m - m_new)
        p = jnp.exp2(s - m_new)
        l = l * corr + jnp.sum(p, axis=0, keepdims=True)
        acc = acc * corr + jnp.dot(vb, p.astype(BF16), preferred_element_type=F32)
        return m_new, l, acc

    init = (jnp.full((1, 2 * tq), NEG_BIG, F32), jnp.zeros((1, 2 * tq), F32),
            jnp.zeros((LANES, 2 * tq), F32))
    m, l, acc = lax.fori_loop(0, qi, lambda kj, c: update(kj, scores(kj), *c), init)
    s = scores(qi)
    key = lax.broadcasted_iota(jnp.int32, s.shape, 0)
    qry = lax.broadcasted_iota(jnp.int32, s.shape, 1) & (tq - 1)
    m, l, acc = update(qi, jnp.where(key <= qry, s, NEG_BIG), m, l, acc)
    lam = _lambda(lq1_ref, lk1_ref, lq2_ref, lk2_ref, lam_init)
    o = acc[:, :tq] / l[:, :tq] - lam * (acc[:, tq:] / l[:, tq:])
    y = o * lax.rsqrt(jnp.mean(o * o, axis=0, keepdims=True) + EPS) * gs_ref[...] * (1.0 - lam_init)
    o_ref[...] = y.T.astype(BF16)


def _flash_diff_attn(lams, g_sub_col, qt, k, vt, b, t, tq, lam_init):
    assert tq & (tq - 1) == 0 and t % tq == 0
    nq = t // tq
    n = b * t
    vec = lambda shape: pl.BlockSpec(shape, lambda bi, h, qi: (0, 0))
    return pl.pallas_call(
        functools.partial(_flash_kernel, tq=tq, lam_init=lam_init),
        grid=(b, A_HEADS, nq),
        in_specs=[vec((1, A_HD))] * 4 + [
            vec((LANES, 1)),
            pl.BlockSpec((1, LANES, tq), lambda bi, h, qi: (bi, h, qi)),
            pl.BlockSpec((t, LANES), lambda bi, h, qi: (bi, h)),
            pl.BlockSpec((1, LANES, t), lambda bi, h, qi: (bi, h, 0)),
        ],
        out_specs=pl.BlockSpec((tq, LANES), lambda bi, h, qi: (bi * nq + qi, h)),
        out_shape=jax.ShapeDtypeStruct((n, A_HEADS * LANES), BF16),
        compiler_params=_cparams(("arbitrary", "arbitrary", "arbitrary")),
        name="flash_diff_attn",
    )(*lams, g_sub_col, qt, k, vt)


def _paged_kernel(pt_ref, lq1_ref, lk1_ref, lq2_ref, lk2_ref, gs_ref, q_ref, kn_ref, vn_ref, *rest,
                  pages, lam_init):
    k_refs = rest[:pages]
    v_refs = rest[pages:2 * pages]
    o_ref = rest[2 * pages]
    qd_ref, m_ref, l_ref, acc_ref = rest[2 * pages + 1:]
    j = pl.program_id(1)
    rows = 2 * A_HEADS
    width = A_HEADS * LANES

    @pl.when(j == 0)
    def _():
        q = jnp.broadcast_to(q_ref[0].astype(F32), (rows, width))
        row = lax.broadcasted_iota(jnp.int32, (rows, width), 0)
        lane = lax.broadcasted_iota(jnp.int32, (rows, width), 1)
        qd = jnp.where(lane // A_HD == row, q, 0.0)
        qd_ref[...] = qd.astype(BF16)
        s_self = jnp.sum(qd * kn_ref[0], axis=-1, keepdims=True)
        m_ref[...] = jnp.broadcast_to(s_self, (rows, LANES))
        l_ref[...] = jnp.ones((rows, LANES), F32)
        acc_ref[...] = jnp.broadcast_to(vn_ref[0], (rows, width))

    qd = qd_ref[...]
    s = jnp.concatenate([jnp.dot(qd, k_refs[r][0].astype(BF16), preferred_element_type=F32)
                         for r in range(pages)], axis=1)
    m = m_ref[:, 0:1]
    m_new = jnp.maximum(m, jnp.max(s, axis=-1, keepdims=True))
    corr = jnp.exp2(m - m_new)
    p = jnp.exp2(s - m_new)
    l = l_ref[:, 0:1] * corr + jnp.sum(p, axis=-1, keepdims=True)
    m_ref[...] = jnp.broadcast_to(m_new, (rows, LANES))
    l_ref[...] = jnp.broadcast_to(l, (rows, LANES))
    pb = p.astype(BF16)
    page = v_refs[0].shape[1] // A_HEADS
    for h in range(A_HEADS):
        vh = jnp.concatenate([v_refs[r][0, pl.ds(h, page, stride=A_HEADS), :].astype(BF16)
                              for r in range(pages)], axis=0)
        cols = slice(h * LANES, (h + 1) * LANES)
        acc_ref[:, cols] = acc_ref[:, cols] * corr + jnp.dot(pb, vh, preferred_element_type=F32)

    @pl.when(j == pl.num_programs(1) - 1)
    def _():
        lam = _lambda(lq1_ref, lk1_ref, lq2_ref, lk2_ref, lam_init)
        g = gs_ref[...]
        outs = []
        for h in range(A_HEADS):
            cols = slice(h * LANES, (h + 1) * LANES)
            a0 = acc_ref[2 * h:2 * h + 1, cols] / l[2 * h:2 * h + 1]
            a1 = acc_ref[2 * h + 1:2 * h + 2, cols] / l[2 * h + 1:2 * h + 2]
            outs.append(_subln(a0 - lam * a1, g, lam_init))
        o_ref[0] = jnp.concatenate(outs, axis=1).astype(BF16)


def _paged_diff_attn(page_table, lams, g_sub, q, k_new, v_new, cache_kt, cache_v, pages, lam_init):
    db, n_pages = page_table.shape
    width, page = cache_kt.shape[1:]
    assert n_pages % pages == 0 and cache_v.shape[1:] == (page * A_HEADS, LANES)
    rows = 2 * A_HEADS
    vec = lambda shape: pl.BlockSpec(shape, lambda bi, j, pt: (0, 0))
    row3 = pl.BlockSpec((1, 1, width), lambda bi, j, pt: (bi, 0, 0))

    def page_spec(r, shape):
        return pl.BlockSpec((1,) + shape, lambda bi, j, pt: (pt[bi, j * pages + r], 0, 0))

    grid_spec = pltpu.PrefetchScalarGridSpec(
        num_scalar_prefetch=1,
        grid=(db, n_pages // pages),
        in_specs=[vec((1, A_HD))] * 4 + [vec((1, LANES)), row3, row3, row3]
        + [page_spec(r, (width, page)) for r in range(pages)]
        + [page_spec(r, (page * A_HEADS, LANES)) for r in range(pages)],
        out_specs=row3,
        scratch_shapes=[pltpu.VMEM((rows, width), BF16), pltpu.VMEM((rows, LANES), F32),
                        pltpu.VMEM((rows, LANES), F32), pltpu.VMEM((rows, width), F32)],
    )
    return pl.pallas_call(
        functools.partial(_paged_kernel, pages=pages, lam_init=lam_init),
        grid_spec=grid_spec,
        out_shape=jax.ShapeDtypeStruct((db, 1, width), BF16),
        compiler_params=_cparams(("arbitrary", "arbitrary")),
        name="paged_diff_attn",
    )(page_table, *lams, g_sub, q, k_new, v_new, *([cache_kt] * pages), *([cache_v] * pages))


def _head_mask(x, e):
    lane = lax.broadcasted_iota(jnp.int32, x.shape, x.ndim - 1)
    keep = (lane >= B_DK) if e else (lane < B_DK)
    return jnp.where(keep, x, jnp.zeros_like(x))


def _rms_plain(o):
    return o * lax.rsqrt(jnp.mean(o * o, axis=-1, keepdims=True) + EPS)


def _retention_kernel(qk_ref, v_ref, gs_ref, din_ref, dq_ref, dk_ref, dc_ref, o_ref, s_out_ref, s_ref):
    ci = pl.program_id(1)

    @pl.when(ci == 0)
    def _():
        s_ref[...] = jnp.zeros_like(s_ref)

    half = B_HEADS * B_DK
    for p in range(B_HEADS // 2):
        q2 = qk_ref[:, p * LANES:(p + 1) * LANES]
        k2 = qk_ref[:, half + p * LANES:half + (p + 1) * LANES]
        s_old = s_ref[p]
        s_bf = s_old.astype(BF16)
        kd = (k2.astype(F32) * dk_ref[p]).astype(BF16)
        new_rows = []
        for e in range(2):
            h = 2 * p + e
            qz = _head_mask(q2, e)
            vh = v_ref[:, h * LANES:(h + 1) * LANES]
            attn = lax.dot_general(qz, k2, NT_DIMS, preferred_element_type=F32) * din_ref[h]
            o = (jnp.dot(attn.astype(BF16), vh, preferred_element_type=F32)
                 + jnp.dot(qz, s_bf, preferred_element_type=F32) * dq_ref[h])
            u = lax.dot_general(kd, vh, TN_DIMS, preferred_element_type=F32)
            rows = slice(e * B_DK, (e + 1) * B_DK)
            new_rows.append(s_old[rows] * dc_ref[p][rows] + u[rows])
            gate = gs_ref[:, h * LANES:(h + 1) * LANES].astype(F32)
            o_ref[:, h * LANES:(h + 1) * LANES] = (_rms_plain(o) * gate).astype(BF16)
        s_ref[p] = jnp.concatenate(new_rows, axis=0)

    @pl.when(ci == pl.num_programs(1) - 1)
    def _():
        s_out_ref[0] = s_ref[...].reshape(B_HEADS, B_DK, B_DV)


def _retention_prompt(qk, v, gs, b, t):
    c = math.gcd(t, RET_CHUNK)
    nc = t // c
    n = b * t
    d_in, d_q, d_k, d_c = _retention_tables(c)
    rows = lambda bi, ci: (bi * nc + ci, 0)
    const3 = lambda bi, ci: (0, 0, 0)
    width = B_HEADS * B_DV
    return pl.pallas_call(
        _retention_kernel,
        grid=(b, nc),
        in_specs=[
            pl.BlockSpec((c, 2 * B_HEADS * B_DK), rows),
            pl.BlockSpec((c, width), rows),
            pl.BlockSpec((c, width), rows),
            pl.BlockSpec(d_in.shape, const3),
            pl.BlockSpec(d_q.shape, const3),
            pl.BlockSpec(d_k.shape, const3),
            pl.BlockSpec(d_c.shape, const3),
        ],
        out_specs=[
            pl.BlockSpec((c, width), rows),
            pl.BlockSpec((1, B_HEADS, B_DK, B_DV), lambda bi, ci: (bi, 0, 0, 0)),
        ],
        out_shape=[jax.ShapeDtypeStruct((n, width), BF16),
                   jax.ShapeDtypeStruct((b, B_HEADS, B_DK, B_DV), F32)],
        scratch_shapes=[pltpu.VMEM((B_HEADS // 2, 2 * B_DK, B_DV), F32)],
        compiler_params=_cparams(("arbitrary", "arbitrary")),
        name="retention_prompt",
    )(qk, v, gs, d_in, d_q, d_k, d_c)


def _retention_step_kernel(qk_ref, v_ref, gs_ref, s0_ref, dq_ref, dc_ref, o_ref, s_out_ref):
    half = B_HEADS * B_DK
    pad = 16
    row0 = lax.broadcasted_iota(jnp.int32, (pad, LANES), 0) == 0
    qk = qk_ref[0].astype(F32)
    v = v_ref[0].astype(F32)

    def pad_rows(x):
        return jnp.where(row0, jnp.broadcast_to(x, (pad, LANES)), 0.0).astype(BF16)

    for p in range(B_HEADS // 2):
        q2 = qk[:, p * LANES:(p + 1) * LANES]
        k2 = qk[:, half + p * LANES:half + (p + 1) * LANES]
        s_old = s0_ref[0, 2 * p:2 * p + 2].reshape(2 * B_DK, B_DV)
        s_bf = s_old.astype(BF16)
        k8 = pad_rows(k2)
        new_rows = []
        for e in range(2):
            h = 2 * p + e
            qz = _head_mask(q2, e)
            vh = v[:, h * LANES:(h + 1) * LANES]
            qk_dot = jnp.sum(qz * k2, axis=-1, keepdims=True)
            qs = jnp.dot(pad_rows(qz), s_bf, preferred_element_type=F32)[0:1]
            o = qk_dot * vh + qs * dq_ref[h][0:1]
            u = lax.dot_general(k8, pad_rows(vh), TN_DIMS, preferred_element_type=F32)
            rows = slice(e * B_DK, (e + 1) * B_DK)
            new_rows.append(s_old[rows] * dc_ref[p][rows] + u[rows])
            gate = gs_ref[0][:, h * LANES:(h + 1) * LANES].astype(F32)
            o_ref[0, :, h * LANES:(h + 1) * LANES] = (_rms_plain(o) * gate).astype(BF16)
        s_out_ref[0, 2 * p:2 * p + 2] = jnp.concatenate(new_rows, axis=0).reshape(2, B_DK, B_DV)


def _retention_sample(qk, v, gs, s0):
    db = qk.shape[0]
    _, d_q, _, d_c = _retention_tables(1)
    width = B_HEADS * B_DV
    row3 = lambda w: pl.BlockSpec((1, 1, w), lambda bi: (bi, 0, 0))
    state = pl.BlockSpec((1, B_HEADS, B_DK, B_DV), lambda bi: (bi, 0, 0, 0))
    const3 = lambda bi: (0, 0, 0)
    return pl.pallas_call(
        _retention_step_kernel,
        grid=(db,),
        in_specs=[row3(2 * B_HEADS * B_DK), row3(width), row3(width), state,
                  pl.BlockSpec(d_q.shape, const3), pl.BlockSpec(d_c.shape, const3)],
        out_specs=[row3(width), state],
        out_shape=[jax.ShapeDtypeStruct((db, 1, width), BF16),
                   jax.ShapeDtypeStruct(s0.shape, F32)],
        compiler_params=_cparams(("arbitrary",)),
        name="retention_sample",
    )(qk, v, gs, s0, d_q, d_c)


def _rms_matmul_kernel(x_ref, g_ref, w_ref, of_ref, ob_ref, xn_ref):
    @pl.when(pl.program_id(1) == 0)
    def _():
        x = x_ref[...]
        xn = x * lax.rsqrt(jnp.mean(x * x, axis=-1, keepdims=True) + EPS) * g_ref[...]
        xn_ref[...] = xn.astype(BF16)

    r = jnp.dot(xn_ref[...], w_ref[...], preferred_element_type=F32)
    of_ref[...] = r
    ob_ref[...] = r.astype(BF16)


def _rms_matmul(x, g, w, tm, tn):
    n, d = x.shape
    width = w.shape[1]
    return pl.pallas_call(
        _rms_matmul_kernel,
        grid=(n // tm, width // tn),
        in_specs=[pl.BlockSpec((tm, d), lambda i, j: (i, 0)),
                  pl.BlockSpec((1, d), lambda i, j: (0, 0)),
                  pl.BlockSpec((d, tn), lambda i, j: (0, j))],
        out_specs=[pl.BlockSpec((tm, tn), lambda i, j: (i, j))] * 2,
        out_shape=[jax.ShapeDtypeStruct((n, width), F32), jax.ShapeDtypeStruct((n, width), BF16)],
        scratch_shapes=[pltpu.VMEM((tm, d), BF16)],
        compiler_params=_cparams(("arbitrary", "arbitrary")),
        name="mem_kv",
    )(x, g, w)


def _softmax_rows(s):
    m = jnp.max(s, axis=-1, keepdims=True)
    p = jnp.exp(s - m)
    return p / jnp.sum(p, axis=-1, keepdims=True)


def _mem_attn_kernel(q_ref, mk_ref, mv_ref, o_ref):
    for h in range(C_HEADS):
        cols = slice(h * C_HD, (h + 1) * C_HD)
        s = lax.dot_general(q_ref[:, cols], mk_ref[:, cols], NT_DIMS, preferred_element_type=F32)
        p = _softmax_rows(s)
        o_ref[:, cols] = jnp.dot(p.astype(BF16), mv_ref[:, cols], preferred_element_type=F32).astype(BF16)


def _mem_attn_prompt(q, mk, mv, t, tm):
    n, width = q.shape
    mem_len = mk.shape[0] // (n // t)
    per_seq = t // tm
    return pl.pallas_call(
        _mem_attn_kernel,
        grid=(n // tm,),
        in_specs=[pl.BlockSpec((tm, width), lambda i: (i, 0)),
                  pl.BlockSpec((mem_len, width), lambda i: (i // per_seq, 0)),
                  pl.BlockSpec((mem_len, width), lambda i: (i // per_seq, 0))],
        out_specs=pl.BlockSpec((tm, width), lambda i: (i, 0)),
        out_shape=jax.ShapeDtypeStruct((n, width), BF16),
        compiler_params=_cparams(("arbitrary",)),
        name="mem_attn_prompt",
    )(q, mk, mv)


def _mem_attn_step_kernel(q_ref, mk_ref, mv_ref, o_ref):
    pad = 16
    width = C_HEADS * C_HD
    q = jnp.broadcast_to(q_ref[0].astype(F32), (pad, width))
    row = lax.broadcasted_iota(jnp.int32, (pad, width), 0)
    lane = lax.broadcasted_iota(jnp.int32, (pad, width), 1)
    qd = jnp.where(lane // C_HD == row, q, 0.0).astype(BF16)
    s = lax.dot_general(qd, mk_ref[0].astype(BF16), NT_DIMS, preferred_element_type=F32)
    p = _softmax_rows(s)
    o = jnp.dot(p.astype(BF16), mv_ref[0].astype(BF16), preferred_element_type=F32)
    o_ref[0] = jnp.concatenate([o[h:h + 1, h * C_HD:(h + 1) * C_HD] for h in range(C_HEADS)],
                               axis=1).astype(BF16)


def _mem_attn_sample(q, mk, mv):
    db, mem_len, width = mk.shape
    row3 = pl.BlockSpec((1, 1, width), lambda bi: (bi, 0, 0))
    mem = pl.BlockSpec((1, mem_len, width), lambda bi: (bi, 0, 0))
    return pl.pallas_call(
        _mem_attn_step_kernel,
        grid=(db,),
        in_specs=[row3, mem, mem],
        out_specs=row3,
        out_shape=jax.ShapeDtypeStruct((db, 1, width), BF16),
        compiler_params=_cparams(("arbitrary",)),
        name="mem_attn_sample",
    )(q, mk, mv)


def _merge_kernel(x_ref, oa_ref, ob_ref, oc_ref, gates_ref, pa_ref, pb_ref, pc_ref, wo_ref, h_ref):
    d = x_ref.shape[1]
    merged = (gates_ref[:, 0:d].astype(F32) * jnp.dot(oa_ref[...], pa_ref[...], preferred_element_type=F32)
              + gates_ref[:, d:2 * d].astype(F32) * jnp.dot(ob_ref[...], pb_ref[...], preferred_element_type=F32)
              + gates_ref[:, 2 * d:3 * d].astype(F32) * jnp.dot(oc_ref[...], pc_ref[...], preferred_element_type=F32))
    h_ref[...] = x_ref[...] + jnp.dot(merged.astype(BF16), wo_ref[...], preferred_element_type=F32)


def _merge(x, oa, ob, oc, gates, p_a, p_b, p_c, w_o, tm):
    n, d = x.shape
    rows = lambda w: pl.BlockSpec((tm, w), lambda i: (i, 0))
    whole = lambda a: _resident(a.shape, lambda i: (0, 0))
    return pl.pallas_call(
        _merge_kernel,
        grid=(n // tm,),
        in_specs=[rows(d), rows(oa.shape[1]), rows(ob.shape[1]), rows(oc.shape[1]), rows(gates.shape[1]),
                  whole(p_a), whole(p_b), whole(p_c), whole(w_o)],
        out_specs=rows(d),
        out_shape=jax.ShapeDtypeStruct((n, d), F32),
        compiler_params=_cparams(("arbitrary",)),
        name="merge",
    )(x, oa, ob, oc, gates, p_a, p_b, p_c, w_o)


def _mlp_kernel(h_ref, g_ref, wu_ref, wd_ref, gf_ref, y_ref, hn_ref, acc_ref):
    f = pl.program_id(1)

    @pl.when(f == 0)
    def _():
        h = h_ref[...]
        hn = h * lax.rsqrt(jnp.mean(h * h, axis=-1, keepdims=True) + EPS) * g_ref[...]
        hn_ref[...] = hn.astype(BF16)
        acc_ref[...] = jnp.zeros_like(acc_ref)

    u = jnp.dot(hn_ref[...], wu_ref[...], preferred_element_type=F32)
    a = jnp.square(jnp.maximum(u, 0.0)).astype(BF16)
    acc_ref[...] += jnp.dot(a, wd_ref[...], preferred_element_type=F32)

    @pl.when(f == pl.num_programs(1) - 1)
    def _():
        h2 = h_ref[...] + acc_ref[...]
        y_ref[...] = h2 * lax.rsqrt(jnp.mean(h2 * h2, axis=-1, keepdims=True) + EPS) * gf_ref[...]


def _mlp(h, g_mlp, w_up, w_down, g_final, tm, tf):
    n, d = h.shape
    d_ff = w_up.shape[1]
    return pl.pallas_call(
        _mlp_kernel,
        grid=(n // tm, d_ff // tf),
        in_specs=[pl.BlockSpec((tm, d), lambda i, f: (i, 0)),
                  pl.BlockSpec((1, d), lambda i, f: (0, 0)),
                  pl.BlockSpec((d, tf), lambda i, f: (0, f)),
                  pl.BlockSpec((tf, d), lambda i, f: (f, 0)),
                  pl.BlockSpec((1, d), lambda i, f: (0, 0))],
        out_specs=pl.BlockSpec((tm, d), lambda i, f: (i, 0)),
        out_shape=jax.ShapeDtypeStruct((n, d), F32),
        scratch_shapes=[pltpu.VMEM((tm, d), BF16), pltpu.VMEM((tm, d), F32)],
        compiler_params=_cparams(("arbitrary", "arbitrary")),
        name="mlp",
    )(h, g_mlp, w_up, w_down, g_final)


def _row_tile(n, target):
    return min(n, target)


def kernel(x_prompt, x_sample, cache_diff_k, cache_diff_v, state_ret, cache_mem_k, cache_mem_v, page_table, mem_prompt, g_mix, w_in, lam_q1, lam_k1, lam_q2, lam_k2, g_subln, g_mem, w_mem_k, w_mem_v, p_a, p_b, p_c, w_o, g_mlp, w_up, w_down, g_final):
    depth = g_mix.shape[0]
    assert depth == 1
    layer = 0
    lam_init = 0.8 - 0.6 * math.exp(-0.3 * layer)
    b, t, d = x_prompt.shape
    db, dt, _ = x_sample.shape
    assert dt == 1
    n_pool, page = cache_diff_k.shape[1:3]
    past_len = page_table.shape[1] * page
    mem_len = mem_prompt.shape[1]
    n_p, n_s = b * t, db * dt

    row = lambda a: a[layer].reshape(1, -1)
    lams = (row(lam_q1), row(lam_k1), row(lam_q2), row(lam_k2))
    g_sub = row(g_subln)
    w_in_b = w_in[layer].astype(BF16)
    w_mem_b = jnp.concatenate([w_mem_k[layer], w_mem_v[layer]], axis=1).astype(BF16)
    pa_b, pb_b, pc_b, wo_b = (a[layer].astype(BF16) for a in (p_a, p_b, p_c, w_o))
    wu_b, wd_b = w_up[layer].astype(BF16), w_down[layer].astype(BF16)
    g_final2 = g_final.reshape(1, -1)

    tm_p = _row_tile(t, 512)
    xp = x_prompt.reshape(n_p, d)
    tab_p = _rope_lane_tables(jnp.arange(t))
    qt, ktf, kab, vaf, vtb, qkb, vb, gsb, qc, gates = _inproj(xp, row(g_mix), w_in_b, tab_p, tm_p, seq_len=t)
    oa = _flash_diff_attn(lams, g_sub.reshape(-1, 1), qt, kab, vtb, b, t, _row_tile(t, 512), lam_init)
    ob, state_p = _retention_prompt(qkb, vb, gsb, b, t)
    mem_f, mem_b = _rms_matmul(mem_prompt.reshape(b * mem_len, d), row(g_mem), w_mem_b,
                               _row_tile(b * mem_len, 256), 1024)
    c_w = C_HEADS * C_HD
    oc = _mem_attn_prompt(qc, mem_b[:, :c_w], mem_b[:, c_w:], t, tm_p)
    hp = _merge(xp, oa, ob, oc, gates, pa_b, pb_b, pc_b, wo_b, _row_tile(t, 256))
    y_p = _mlp(hp, row(g_mlp), wu_b, wd_b, g_final2, tm_p, 1024)

    xs = x_sample.reshape(n_s, d)
    tab_s = jnp.tile(_rope_lane_tables(past_len + jnp.arange(dt)), (db, 1))
    qa_s, kaf_s, vaf_s, qkb_s, vb_s, gsb_s, qc_s, gates_s = _inproj(xs, row(g_mix), w_in_b, tab_s, n_s)
    width_a = A_HEADS * LANES
    r3 = lambda a: a.reshape(db, 1, a.shape[-1])
    ckt = cache_diff_k[layer].transpose(0, 2, 3, 4, 1).reshape(n_pool, width_a, page)
    cv = cache_diff_v[layer].reshape(n_pool, page * A_HEADS, LANES)
    oa_s = _paged_diff_attn(page_table, lams, g_sub, r3(qa_s), r3(kaf_s), r3(vaf_s), ckt, cv, 8, lam_init)
    ob_s, state_s = _retention_sample(r3(qkb_s), r3(vb_s), r3(gsb_s), state_ret[layer])
    oc_s = _mem_attn_sample(r3(qc_s), cache_mem_k[layer].reshape(db, mem_len, c_w),
                            cache_mem_v[layer].reshape(db, mem_len, c_w))
    hs = _merge(xs, oa_s.reshape(n_s, -1), ob_s.reshape(n_s, -1), oc_s.reshape(n_s, -1), gates_s,
                pa_b, pb_b, pc_b, wo_b, n_s)
    y_s = _mlp(hs, row(g_mlp), wu_b, wd_b, g_final2, n_s, 1024)

    return (
        y_p.reshape(b, t, d),
        y_s.reshape(db, dt, d),
        ktf.reshape(1, b, A_HEADS, 2, A_HD, t).transpose(0, 1, 5, 2, 3, 4),
        vaf.reshape(1, b, t, A_HEADS, 2 * A_HD),
        state_p[None],
        mem_f[:, :c_w].reshape(1, b, mem_len, C_HEADS, C_HD),
        mem_f[:, c_w:].reshape(1, b, mem_len, C_HEADS, C_HD),
        kaf_s.reshape(1, db, dt, A_HEADS, 2, A_HD),
        vaf_s.reshape(1, db, dt, A_HEADS, 2 * A_HD),
        state_s[None],
    )
```

```python
import functools
import math

import jax
import jax.numpy as jnp
from jax import lax
from jax.experimental import pallas as pl
from jax.experimental.pallas import tpu as pltpu

F32 = jnp.float32
BF16 = jnp.bfloat16
EPS = 1e-6
NEG_BIG = -1e30
LOG2E = math.log2(math.e)
FLASH_CHAINS = 4

LANES = 128
VMEM_LIMIT = 56 * 1024 * 1024

A_HEADS = 8
A_HD = 64
A_ROT = A_HD // 4
ROPE_THETA = 500000.0
B_HEADS = 8
B_DK = 64
B_DV = 128
RET_THETA = 10000.0
RET_CHUNK = 128
C_HEADS = 4
C_HD = 256
SEG = 1024
N_SEG = 7

NT_DIMS = (((1,), (1,)), ((), ()))
TN_DIMS = (((0,), (0,)), ((), ()))


def _cparams(sem):
    return pltpu.CompilerParams(dimension_semantics=sem, vmem_limit_bytes=VMEM_LIMIT)


def _resident(shape, index_map):
    return pl.BlockSpec(shape, index_map, pipeline_mode=pl.Buffered(1))


def _rope_angles(pos, n_rot, theta):
    inv = 1.0 / (theta ** (jnp.arange(0, n_rot, 2, dtype=F32) / n_rot))
    ang = pos.astype(F32)[:, None] * inv[None, :]
    return jnp.cos(ang), jnp.sin(ang)


def _rope_lane_tables(pos):
    t = pos.shape[0]
    ca, sa = _rope_angles(pos, A_ROT, ROPE_THETA)
    ha = A_ROT // 2
    c_a = jnp.concatenate([ca, ca, jnp.ones((t, A_HD - A_ROT), F32)], axis=1)
    p_a = jnp.concatenate([jnp.zeros((t, ha), F32), sa, jnp.zeros((t, A_HD - A_ROT), F32)], axis=1)
    n_a = jnp.concatenate([-sa, jnp.zeros((t, A_HD - ha), F32)], axis=1)
    cb, sb = _rope_angles(pos, B_DK, RET_THETA)
    hb = B_DK // 2
    c_b = jnp.concatenate([cb, cb], axis=1)
    p_b = jnp.concatenate([jnp.zeros((t, hb), F32), sb], axis=1)
    n_b = jnp.concatenate([-sb, jnp.zeros((t, hb), F32)], axis=1)
    return jnp.concatenate([jnp.tile(a, (1, 2)) for a in (c_a, p_a, n_a, c_b, p_b, n_b)], axis=1)


def _retention_tables(c):
    log_g = jnp.log(1.0 - 2.0 ** (-5.0 - jnp.arange(B_HEADS, dtype=F32)))
    idx = jnp.arange(c, dtype=F32)
    rel = idx[:, None] - idx[None, :]
    d_in = jnp.where(rel >= 0, jnp.exp(log_g[:, None, None] * jnp.maximum(rel, 0.0)), 0.0)
    d_q = jnp.exp(log_g[:, None] * (idx + 1.0))
    d_k = jnp.exp(log_g[:, None] * (c - 1.0 - idx))
    d_c = jnp.exp(log_g * c)
    d_q = jnp.broadcast_to(d_q[:, :, None], (B_HEADS, c, LANES))
    d_k = jnp.broadcast_to(d_k[:, :, None], (B_HEADS, c, B_DK)).reshape(B_HEADS // 2, 2, c, B_DK)
    d_k = d_k.transpose(0, 2, 1, 3).reshape(B_HEADS // 2, c, 2 * B_DK)
    d_c = jnp.broadcast_to(d_c[:, None, None], (B_HEADS, B_DK, LANES)).reshape(B_HEADS // 2, 2 * B_DK, LANES)
    return d_in, d_q, d_k, d_c


def _rope_cols(r, c, p, n, half):
    outs = []
    for h in range(r.shape[1] // LANES):
        blk = r[:, h * LANES:(h + 1) * LANES]
        outs.append(blk * c + pltpu.roll(blk, half, 1) * p + pltpu.roll(blk, LANES - half, 1) * n)
    return jnp.concatenate(outs, axis=1)


def _sigmoid(x):
    return 1.0 / (1.0 + jnp.exp(-x))


def _inproj_kernel(x_ref, g_ref, w_ref, tab_ref, *refs, token_minor, n_col):
    xn_ref, ra_ref, rb_ref = refs[-3:]
    if token_minor:
        qa_ref, kaf_ref, kab_ref, vaf_ref, vtb_ref, qkb_ref, vb_ref, gb_ref, qc_ref, gates_ref = refs[:-3]
    else:
        qa_ref, kaf_ref, vaf_ref, qkb_ref, vb_ref, gb_ref, qc_ref, gates_ref = refs[:-3]
    j = pl.program_id(1)
    r_refs = (ra_ref, rb_ref)

    def matmul(slot):
        r_refs[slot][...] = jnp.dot(xn_ref[...], w_ref[...], preferred_element_type=F32)

    def rope_a(v):
        return _rope_cols(v, tab_ref[:, 0:LANES], tab_ref[:, LANES:2 * LANES],
                          tab_ref[:, 2 * LANES:3 * LANES], A_ROT // 2)

    def rope_b(v):
        return _rope_cols(v, tab_ref[:, 3 * LANES:4 * LANES], tab_ref[:, 4 * LANES:5 * LANES],
                          tab_ref[:, 5 * LANES:6 * LANES], B_DK // 2)

    def finish_q(r):
        qa = rope_a(r) * (A_HD ** -0.5 * LOG2E)
        if token_minor:
            qa_ref[0] = qa.T.astype(BF16)
        else:
            qa_ref[...] = qa.astype(BF16)

    def finish_k(r):
        ka = rope_a(r)
        if token_minor:
            kaf_ref[0] = ka.T
            kab_ref[...] = ka.astype(BF16)
        else:
            kaf_ref[...] = ka

    def finish_v(r):
        vaf_ref[...] = r
        if token_minor:
            vtb_ref[0] = r.T.astype(BF16)

    def finish_qkb(r):
        rb = rope_b(r)
        half = SEG // 2
        qkb_ref[:, :half] = rb[:, :half].astype(BF16)
        qkb_ref[:, half:] = (rb[:, half:] * (B_DK ** -0.5)).astype(BF16)

    def finish_vb(r):
        vb_ref[...] = r.astype(BF16)

    def finish_gb(r):
        gb_ref[...] = (r * _sigmoid(r)).astype(BF16)

    def finish_qc(r):
        qc_ref[...] = (r * (C_HD ** -0.5)).astype(BF16)

    def finish_gates(r):
        gates_ref[...] = _sigmoid(r).astype(BF16)

    finish = (finish_q, finish_k, finish_v, finish_qkb, finish_vb, finish_gb, finish_qc)

    @pl.when(j == 0)
    def _():
        x = x_ref[...]
        xn = x * lax.rsqrt(jnp.mean(x * x, axis=-1, keepdims=True) + EPS) * g_ref[...]
        xn_ref[...] = xn.astype(BF16)
        matmul(0)

    for k in range(1, N_SEG + 1):
        @pl.when(j == k)
        def _(k=k):
            matmul(k % 2)
            finish[k - 1](r_refs[(k - 1) % 2][...])

    for parity in range(2):
        @pl.when((j > N_SEG) & (j < n_col) & (j % 2 == parity))
        def _(parity=parity):
            matmul(parity)
            finish_gates(r_refs[1 - parity][...])

    @pl.when(j == n_col)
    def _():
        finish_gates(r_refs[(n_col - 1) % 2][...])


def _inproj(x, g, w, tab, tm, seq_len=None):
    n, d = x.shape
    n_col = w.shape[1] // SEG
    n_gate = n_col - N_SEG
    tab_blocks = tab.shape[0] // tm
    token_minor = seq_len is not None
    rows = lambda dt: (jax.ShapeDtypeStruct((n, SEG), dt), pl.BlockSpec((tm, SEG), lambda i, j: (i, 0)))
    if token_minor:
        per_seq = seq_len // tm
        tmin = lambda dt: (jax.ShapeDtypeStruct((n // seq_len, SEG, seq_len), dt),
                           pl.BlockSpec((1, SEG, tm), lambda i, j: (i // per_seq, 0, i % per_seq)))
        outs = [tmin(BF16), tmin(F32), rows(BF16), rows(F32), tmin(BF16)]
    else:
        outs = [rows(BF16), rows(F32), rows(F32)]
    outs += [rows(BF16)] * 4
    outs.append((jax.ShapeDtypeStruct((n, n_gate * SEG), BF16),
                 pl.BlockSpec((tm, SEG), lambda i, j: (i, jnp.clip(j - N_SEG - 1, 0, n_gate - 1)))))
    out_shape = [o[0] for o in outs]
    out_specs = [o[1] for o in outs]
    return pl.pallas_call(
        functools.partial(_inproj_kernel, token_minor=token_minor, n_col=n_col),
        grid=(n // tm, n_col + 1),
        in_specs=[
            pl.BlockSpec((tm, d), lambda i, j: (i, 0)),
            pl.BlockSpec((1, d), lambda i, j: (0, 0)),
            pl.BlockSpec((d, SEG), lambda i, j: (0, jnp.minimum(j, n_col - 1))),
            pl.BlockSpec((tm, 6 * LANES), lambda i, j: (i % tab_blocks, 0)),
        ],
        out_specs=out_specs,
        out_shape=out_shape,
        scratch_shapes=[pltpu.VMEM((tm, d), BF16), pltpu.VMEM((tm, SEG), F32), pltpu.VMEM((tm, SEG), F32)],
        compiler_params=_cparams(("arbitrary", "arbitrary")),
        name="inproj",
    )(x, g, w, tab)


def _lambda(lq1_ref, lk1_ref, lq2_ref, lk2_ref, lam_init):
    a = jnp.sum(lq1_ref[...] * lk1_ref[...], axis=-1, keepdims=True)
    b = jnp.sum(lq2_ref[...] * lk2_ref[...], axis=-1, keepdims=True)
    return jnp.exp(a) - jnp.exp(b) + lam_init


def _subln(o, g, lam_init):
    y = o * lax.rsqrt(jnp.mean(o * o, axis=-1, keepdims=True) + EPS) * g
    return y * (1.0 - lam_init)


def _flash_kernel(lq1_ref, lk1_ref, lq2_ref, lk2_ref, gs_ref, qt_ref, k_ref, vt_ref, o_ref,
                  sa_ref, sb_ref, m_ref, l_ref, acc_ref, *, tq, lam_init):
    qi = pl.program_id(2)
    qt = qt_ref[0]
    feat = lax.broadcasted_iota(jnp.int32, qt.shape, 0)
    zero = jnp.zeros_like(qt)
    qs = jnp.concatenate([jnp.where(feat < A_HD, qt, zero), jnp.where(feat >= A_HD, qt, zero)], axis=1)

    s_refs = (sa_ref, sb_ref)

    def qk(kj, slot):
        start = pl.multiple_of(kj * tq, tq)
        s_refs[slot][...] = jnp.dot(k_ref[pl.ds(start, tq), :], qs, preferred_element_type=F32)

    def upd(kj, slot, diagonal):
        start = pl.multiple_of(kj * tq, tq)
        vb = vt_ref[0, :, pl.ds(start, tq)]
        s = s_refs[slot][...]
        if diagonal:
            key = lax.broadcasted_iota(jnp.int32, s.shape, 0)
            qry = lax.broadcasted_iota(jnp.int32, s.shape, 1) & (tq - 1)
            s = jnp.where(key <= qry, s, NEG_BIG)
        m = m_ref[...]
        m_new = jnp.maximum(m, jnp.max(s, axis=0, keepdims=True))
        corr = jnp.exp2(m - m_new)
        p = jnp.exp2(s - m_new)
        m_ref[...] = m_new
        l_ref[...] = l_ref[...] * corr + jnp.sum(p, axis=0, keepdims=True)
        acc_ref[...] = acc_ref[...] * corr + jnp.dot(vb, p.astype(BF16), preferred_element_type=F32)

    m_ref[...] = jnp.full(m_ref.shape, NEG_BIG, F32)
    l_ref[...] = jnp.zeros(l_ref.shape, F32)
    acc_ref[...] = jnp.zeros(acc_ref.shape, F32)
    qk(0, 0)

    def pair(pi, carry):
        b0 = 2 * pi
        qk(b0 + 1, 1)
        upd(b0, 0, False)
        qk(b0 + 2, 0)
        upd(b0 + 1, 1, False)
        return carry

    lax.fori_loop(0, qi // 2, pair, 0)

    @pl.when(qi % 2 == 1)
    def _():
        qk(qi, 1)
        upd(qi - 1, 0, False)
        upd(qi, 1, True)

    @pl.when(qi % 2 == 0)
    def _():
        upd(qi, 0, True)

    l = l_ref[...]
    acc = acc_ref[...]
    lam = _lambda(lq1_ref, lk1_ref, lq2_ref, lk2_ref, lam_init)
    o = acc[:, :tq] / l[:, :tq] - lam * (acc[:, tq:] / l[:, tq:])
    y = o * lax.rsqrt(jnp.mean(o * o, axis=0, keepdims=True) + EPS) * gs_ref[...] * (1.0 - lam_init)
    o_ref[...] = y.T.astype(BF16)


def _flash_diff_attn(lams, g_sub_col, qt, k, vt, b, t, tq, lam_init):
    assert tq & (tq - 1) == 0 and t % tq == 0
    nq = t // tq
    n = b * t
    vec = lambda shape: pl.BlockSpec(shape, lambda bi, h, qi: (0, 0))
    return pl.pallas_call(
        functools.partial(_flash_kernel, tq=tq, lam_init=lam_init),
        grid=(b, A_HEADS, nq),
        in_specs=[vec((1, A_HD))] * 4 + [
            vec((LANES, 1)),
            pl.BlockSpec((1, LANES, tq), lambda bi, h, qi: (bi, h, qi)),
            pl.BlockSpec((t, LANES), lambda bi, h, qi: (bi, h)),
            pl.BlockSpec((1, LANES, t), lambda bi, h, qi: (bi, h, 0)),
        ],
        out_specs=pl.BlockSpec((tq, LANES), lambda bi, h, qi: (bi * nq + qi, h)),
        out_shape=jax.ShapeDtypeStruct((n, A_HEADS * LANES), BF16),
        scratch_shapes=[pltpu.VMEM((tq, 2 * tq), F32), pltpu.VMEM((tq, 2 * tq), F32),
                        pltpu.VMEM((1, 2 * tq), F32), pltpu.VMEM((1, 2 * tq), F32),
                        pltpu.VMEM((LANES, 2 * tq), F32)],
        compiler_params=_cparams(("arbitrary", "arbitrary", "arbitrary")),
        name="flash_diff_attn",
    )(*lams, g_sub_col, qt, k, vt)


def _paged_kernel(pt_ref, lq1_ref, lk1_ref, lq2_ref, lk2_ref, gs_ref, q_ref, kn_ref, vn_ref, *rest,
                  pages, lam_init):
    k_refs = rest[:pages]
    v_refs = rest[pages:2 * pages]
    o_ref = rest[2 * pages]
    qd_ref, m_ref, l_ref, acc_ref = rest[2 * pages + 1:]
    j = pl.program_id(1)
    rows = 2 * A_HEADS
    width = A_HEADS * LANES

    @pl.when(j == 0)
    def _():
        q = jnp.broadcast_to(q_ref[0].astype(F32), (rows, width))
        row = lax.broadcasted_iota(jnp.int32, (rows, width), 0)
        lane = lax.broadcasted_iota(jnp.int32, (rows, width), 1)
        qd = jnp.where(lane // A_HD == row, q, 0.0)
        qd_ref[...] = qd.astype(BF16)
        s_self = jnp.sum(qd * kn_ref[0], axis=-1, keepdims=True)
        m_ref[...] = jnp.broadcast_to(s_self, (rows, LANES))
        l_ref[...] = jnp.ones((rows, LANES), F32)
        acc_ref[...] = jnp.broadcast_to(vn_ref[0], (rows, width))

    qd = qd_ref[...]
    s = jnp.concatenate([jnp.dot(qd, k_refs[r][0].astype(BF16), preferred_element_type=F32)
                         for r in range(pages)], axis=1)
    m = m_ref[:, 0:1]
    m_new = jnp.maximum(m, jnp.max(s, axis=-1, keepdims=True))
    corr = jnp.exp2(m - m_new)
    p = jnp.exp2(s - m_new)
    l = l_ref[:, 0:1] * corr + jnp.sum(p, axis=-1, keepdims=True)
    m_ref[...] = jnp.broadcast_to(m_new, (rows, LANES))
    l_ref[...] = jnp.broadcast_to(l, (rows, LANES))
    pb = p.astype(BF16)
    page = v_refs[0].shape[1] // A_HEADS
    for h in range(A_HEADS):
        vh = jnp.concatenate([v_refs[r][0, pl.ds(h, page, stride=A_HEADS), :].astype(BF16)
                              for r in range(pages)], axis=0)
        cols = slice(h * LANES, (h + 1) * LANES)
        acc_ref[:, cols] = acc_ref[:, cols] * corr + jnp.dot(pb, vh, preferred_element_type=F32)

    @pl.when(j == pl.num_programs(1) - 1)
    def _():
        lam = _lambda(lq1_ref, lk1_ref, lq2_ref, lk2_ref, lam_init)
        g = gs_ref[...]
        outs = []
        for h in range(A_HEADS):
            cols = slice(h * LANES, (h + 1) * LANES)
            a0 = acc_ref[2 * h:2 * h + 1, cols] / l[2 * h:2 * h + 1]
            a1 = acc_ref[2 * h + 1:2 * h + 2, cols] / l[2 * h + 1:2 * h + 2]
            outs.append(_subln(a0 - lam * a1, g, lam_init))
        o_ref[0] = jnp.concatenate(outs, axis=1).astype(BF16)


def _paged_diff_attn(page_table, lams, g_sub, q, k_new, v_new, cache_kt, cache_v, pages, lam_init):
    db, n_pages = page_table.shape
    width, page = cache_kt.shape[1:]
    assert n_pages % pages == 0 and cache_v.shape[1:] == (page * A_HEADS, LANES)
    rows = 2 * A_HEADS
    vec = lambda shape: pl.BlockSpec(shape, lambda bi, j, pt: (0, 0))
    row3 = pl.BlockSpec((1, 1, width), lambda bi, j, pt: (bi, 0, 0))

    def page_spec(r, shape):
        return pl.BlockSpec((1,) + shape, lambda bi, j, pt: (pt[bi, j * pages + r], 0, 0))

    grid_spec = pltpu.PrefetchScalarGridSpec(
        num_scalar_prefetch=1,
        grid=(db, n_pages // pages),
        in_specs=[vec((1, A_HD))] * 4 + [vec((1, LANES)), row3, row3, row3]
        + [page_spec(r, (width, page)) for r in range(pages)]
        + [page_spec(r, (page * A_HEADS, LANES)) for r in range(pages)],
        out_specs=row3,
        scratch_shapes=[pltpu.VMEM((rows, width), BF16), pltpu.VMEM((rows, LANES), F32),
                        pltpu.VMEM((rows, LANES), F32), pltpu.VMEM((rows, width), F32)],
    )
    return pl.pallas_call(
        functools.partial(_paged_kernel, pages=pages, lam_init=lam_init),
        grid_spec=grid_spec,
        out_shape=jax.ShapeDtypeStruct((db, 1, width), BF16),
        compiler_params=_cparams(("arbitrary", "arbitrary")),
        name="paged_diff_attn",
    )(page_table, *lams, g_sub, q, k_new, v_new, *([cache_kt] * pages), *([cache_v] * pages))


def _head_mask(x, e):
    lane = lax.broadcasted_iota(jnp.int32, x.shape, x.ndim - 1)
    keep = (lane >= B_DK) if e else (lane < B_DK)
    return jnp.where(keep, x, jnp.zeros_like(x))


def _rms_plain(o):
    return o * lax.rsqrt(jnp.mean(o * o, axis=-1, keepdims=True) + EPS)


def _retention_kernel(qk_ref, v_ref, gs_ref, din_ref, dq_ref, dk_ref, dc_ref, o_ref, s_out_ref, s_ref):
    ci = pl.program_id(1)

    @pl.when(ci == 0)
    def _():
        s_ref[...] = jnp.zeros_like(s_ref)

    half = B_HEADS * B_DK
    for p in range(B_HEADS // 2):
        q2 = qk_ref[:, p * LANES:(p + 1) * LANES]
        k2 = qk_ref[:, half + p * LANES:half + (p + 1) * LANES]
        s_old = s_ref[p]
        s_bf = s_old.astype(BF16)
        kd = (k2.astype(F32) * dk_ref[p]).astype(BF16)
        new_rows = []
        for e in range(2):
            h = 2 * p + e
            qz = _head_mask(q2, e)
            vh = v_ref[:, h * LANES:(h + 1) * LANES]
            attn = lax.dot_general(qz, k2, NT_DIMS, preferred_element_type=F32) * din_ref[h]
            o = (jnp.dot(attn.astype(BF16), vh, preferred_element_type=F32)
                 + jnp.dot(qz, s_bf, preferred_element_type=F32) * dq_ref[h])
            u = lax.dot_general(kd, vh, TN_DIMS, preferred_element_type=F32)
            rows = slice(e * B_DK, (e + 1) * B_DK)
            new_rows.append(s_old[rows] * dc_ref[p][rows] + u[rows])
            gate = gs_ref[:, h * LANES:(h + 1) * LANES].astype(F32)
            o_ref[:, h * LANES:(h + 1) * LANES] = (_rms_plain(o) * gate).astype(BF16)
        s_ref[p] = jnp.concatenate(new_rows, axis=0)

    @pl.when(ci == pl.num_programs(1) - 1)
    def _():
        s_out_ref[0] = s_ref[...].reshape(B_HEADS, B_DK, B_DV)


def _retention_prompt(qk, v, gs, b, t):
    c = math.gcd(t, RET_CHUNK)
    nc = t // c
    n = b * t
    d_in, d_q, d_k, d_c = _retention_tables(c)
    rows = lambda bi, ci: (bi * nc + ci, 0)
    const3 = lambda bi, ci: (0, 0, 0)
    width = B_HEADS * B_DV
    return pl.pallas_call(
        _retention_kernel,
        grid=(b, nc),
        in_specs=[
            pl.BlockSpec((c, 2 * B_HEADS * B_DK), rows),
            pl.BlockSpec((c, width), rows),
            pl.BlockSpec((c, width), rows),
            pl.BlockSpec(d_in.shape, const3),
            pl.BlockSpec(d_q.shape, const3),
            pl.BlockSpec(d_k.shape, const3),
            pl.BlockSpec(d_c.shape, const3),
        ],
        out_specs=[
            pl.BlockSpec((c, width), rows),
            pl.BlockSpec((1, B_HEADS, B_DK, B_DV), lambda bi, ci: (bi, 0, 0, 0)),
        ],
        out_shape=[jax.ShapeDtypeStruct((n, width), BF16),
                   jax.ShapeDtypeStruct((b, B_HEADS, B_DK, B_DV), F32)],
        scratch_shapes=[pltpu.VMEM((B_HEADS // 2, 2 * B_DK, B_DV), F32)],
        compiler_params=_cparams(("arbitrary", "arbitrary")),
        name="retention_prompt",
    )(qk, v, gs, d_in, d_q, d_k, d_c)


def _retention_step_kernel(qk_ref, v_ref, gs_ref, s0_ref, dq_ref, dc_ref, o_ref, s_out_ref):
    half = B_HEADS * B_DK
    pad = 16
    row0 = lax.broadcasted_iota(jnp.int32, (pad, LANES), 0) == 0
    qk = qk_ref[0].astype(F32)
    v = v_ref[0].astype(F32)

    def pad_rows(x):
        return jnp.where(row0, jnp.broadcast_to(x, (pad, LANES)), 0.0).astype(BF16)

    for p in range(B_HEADS // 2):
        q2 = qk[:, p * LANES:(p + 1) * LANES]
        k2 = qk[:, half + p * LANES:half + (p + 1) * LANES]
        s_old = s0_ref[0, 2 * p:2 * p + 2].reshape(2 * B_DK, B_DV)
        s_bf = s_old.astype(BF16)
        k8 = pad_rows(k2)
        new_rows = []
        for e in range(2):
            h = 2 * p + e
            qz = _head_mask(q2, e)
            vh = v[:, h * LANES:(h + 1) * LANES]
            qk_dot = jnp.sum(qz * k2, axis=-1, keepdims=True)
            qs = jnp.dot(pad_rows(qz), s_bf, preferred_element_type=F32)[0:1]
            o = qk_dot * vh + qs * dq_ref[h][0:1]
            u = lax.dot_general(k8, pad_rows(vh), TN_DIMS, preferred_element_type=F32)
            rows = slice(e * B_DK, (e + 1) * B_DK)
            new_rows.append(s_old[rows] * dc_ref[p][rows] + u[rows])
            gate = gs_ref[0][:, h * LANES:(h + 1) * LANES].astype(F32)
            o_ref[0, :, h * LANES:(h + 1) * LANES] = (_rms_plain(o) * gate).astype(BF16)
        s_out_ref[0, 2 * p:2 * p + 2] = jnp.concatenate(new_rows, axis=0).reshape(2, B_DK, B_DV)


def _retention_sample(qk, v, gs, s0):
    db = qk.shape[0]
    _, d_q, _, d_c = _retention_tables(1)
    width = B_HEADS * B_DV
    row3 = lambda w: pl.BlockSpec((1, 1, w), lambda bi: (bi, 0, 0))
    state = pl.BlockSpec((1, B_HEADS, B_DK, B_DV), lambda bi: (bi, 0, 0, 0))
    const3 = lambda bi: (0, 0, 0)
    return pl.pallas_call(
        _retention_step_kernel,
        grid=(db,),
        in_specs=[row3(2 * B_HEADS * B_DK), row3(width), row3(width), state,
                  pl.BlockSpec(d_q.shape, const3), pl.BlockSpec(d_c.shape, const3)],
        out_specs=[row3(width), state],
        out_shape=[jax.ShapeDtypeStruct((db, 1, width), BF16),
                   jax.ShapeDtypeStruct(s0.shape, F32)],
        compiler_params=_cparams(("arbitrary",)),
        name="retention_sample",
    )(qk, v, gs, s0, d_q, d_c)


def _rms_matmul_kernel(x_ref, g_ref, w_ref, of_ref, ob_ref, xn_ref):
    @pl.when(pl.program_id(1) == 0)
    def _():
        x = x_ref[...]
        xn = x * lax.rsqrt(jnp.mean(x * x, axis=-1, keepdims=True) + EPS) * g_ref[...]
        xn_ref[...] = xn.astype(BF16)

    r = jnp.dot(xn_ref[...], w_ref[...], preferred_element_type=F32)
    of_ref[...] = r
    ob_ref[...] = r.astype(BF16)


def _rms_matmul(x, g, w, tm, tn):
    n, d = x.shape
    width = w.shape[1]
    return pl.pallas_call(
        _rms_matmul_kernel,
        grid=(n // tm, width // tn),
        in_specs=[pl.BlockSpec((tm, d), lambda i, j: (i, 0)),
                  pl.BlockSpec((1, d), lambda i, j: (0, 0)),
                  pl.BlockSpec((d, tn), lambda i, j: (0, j))],
        out_specs=[pl.BlockSpec((tm, tn), lambda i, j: (i, j))] * 2,
        out_shape=[jax.ShapeDtypeStruct((n, width), F32), jax.ShapeDtypeStruct((n, width), BF16)],
        scratch_shapes=[pltpu.VMEM((tm, d), BF16)],
        compiler_params=_cparams(("arbitrary", "arbitrary")),
        name="mem_kv",
    )(x, g, w)


def _softmax_rows(s):
    m = jnp.max(s, axis=-1, keepdims=True)
    p = jnp.exp(s - m)
    return p / jnp.sum(p, axis=-1, keepdims=True)


def _mem_attn_kernel(q_ref, mk_ref, mv_ref, o_ref):
    for h in range(C_HEADS):
        cols = slice(h * C_HD, (h + 1) * C_HD)
        s = lax.dot_general(q_ref[:, cols], mk_ref[:, cols], NT_DIMS, preferred_element_type=F32)
        p = _softmax_rows(s)
        o_ref[:, cols] = jnp.dot(p.astype(BF16), mv_ref[:, cols], preferred_element_type=F32).astype(BF16)


def _mem_attn_prompt(q, mk, mv, t, tm):
    n, width = q.shape
    mem_len = mk.shape[0] // (n // t)
    per_seq = t // tm
    return pl.pallas_call(
        _mem_attn_kernel,
        grid=(n // tm,),
        in_specs=[pl.BlockSpec((tm, width), lambda i: (i, 0)),
                  pl.BlockSpec((mem_len, width), lambda i: (i // per_seq, 0)),
                  pl.BlockSpec((mem_len, width), lambda i: (i // per_seq, 0))],
        out_specs=pl.BlockSpec((tm, width), lambda i: (i, 0)),
        out_shape=jax.ShapeDtypeStruct((n, width), BF16),
        compiler_params=_cparams(("arbitrary",)),
        name="mem_attn_prompt",
    )(q, mk, mv)


def _mem_attn_step_kernel(q_ref, mk_ref, mv_ref, o_ref):
    pad = 16
    width = C_HEADS * C_HD
    q = jnp.broadcast_to(q_ref[0].astype(F32), (pad, width))
    row = lax.broadcasted_iota(jnp.int32, (pad, width), 0)
    lane = lax.broadcasted_iota(jnp.int32, (pad, width), 1)
    qd = jnp.where(lane // C_HD == row, q, 0.0).astype(BF16)
    s = lax.dot_general(qd, mk_ref[0].astype(BF16), NT_DIMS, preferred_element_type=F32)
    p = _softmax_rows(s)
    o = jnp.dot(p.astype(BF16), mv_ref[0].astype(BF16), preferred_element_type=F32)
    o_ref[0] = jnp.concatenate([o[h:h + 1, h * C_HD:(h + 1) * C_HD] for h in range(C_HEADS)],
                               axis=1).astype(BF16)


def _mem_attn_sample(q, mk, mv):
    db, mem_len, width = mk.shape
    row3 = pl.BlockSpec((1, 1, width), lambda bi: (bi, 0, 0))
    mem = pl.BlockSpec((1, mem_len, width), lambda bi: (bi, 0, 0))
    return pl.pallas_call(
        _mem_attn_step_kernel,
        grid=(db,),
        in_specs=[row3, mem, mem],
        out_specs=row3,
        out_shape=jax.ShapeDtypeStruct((db, 1, width), BF16),
        compiler_params=_cparams(("arbitrary",)),
        name="mem_attn_sample",
    )(q, mk, mv)


def _merge_kernel(x_ref, oa_ref, ob_ref, oc_ref, gates_ref, pa_ref, pb_ref, pc_ref, wo_ref, h_ref):
    d = x_ref.shape[1]
    merged = (gates_ref[:, 0:d].astype(F32) * jnp.dot(oa_ref[...], pa_ref[...], preferred_element_type=F32)
              + gates_ref[:, d:2 * d].astype(F32) * jnp.dot(ob_ref[...], pb_ref[...], preferred_element_type=F32)
              + gates_ref[:, 2 * d:3 * d].astype(F32) * jnp.dot(oc_ref[...], pc_ref[...], preferred_element_type=F32))
    h_ref[...] = x_ref[...] + jnp.dot(merged.astype(BF16), wo_ref[...], preferred_element_type=F32)


def _merge(x, oa, ob, oc, gates, p_a, p_b, p_c, w_o, tm):
    n, d = x.shape
    rows = lambda w: pl.BlockSpec((tm, w), lambda i: (i, 0))
    whole = lambda a: _resident(a.shape, lambda i: (0, 0))
    return pl.pallas_call(
        _merge_kernel,
        grid=(n // tm,),
        in_specs=[rows(d), rows(oa.shape[1]), rows(ob.shape[1]), rows(oc.shape[1]), rows(gates.shape[1]),
                  whole(p_a), whole(p_b), whole(p_c), whole(w_o)],
        out_specs=rows(d),
        out_shape=jax.ShapeDtypeStruct((n, d), F32),
        compiler_params=_cparams(("arbitrary",)),
        name="merge",
    )(x, oa, ob, oc, gates, p_a, p_b, p_c, w_o)


def _mlp_kernel(h_ref, g_ref, wu_ref, wd_ref, gf_ref, y_ref, hn_ref, acc_ref):
    f = pl.program_id(1)

    @pl.when(f == 0)
    def _():
        h = h_ref[...]
        hn = h * lax.rsqrt(jnp.mean(h * h, axis=-1, keepdims=True) + EPS) * g_ref[...]
        hn_ref[...] = hn.astype(BF16)
        acc_ref[...] = jnp.zeros_like(acc_ref)

    u = jnp.dot(hn_ref[...], wu_ref[...], preferred_element_type=F32)
    a = jnp.square(jnp.maximum(u, 0.0)).astype(BF16)
    acc_ref[...] += jnp.dot(a, wd_ref[...], preferred_element_type=F32)

    @pl.when(f == pl.num_programs(1) - 1)
    def _():
        h2 = h_ref[...] + acc_ref[...]
        y_ref[...] = h2 * lax.rsqrt(jnp.mean(h2 * h2, axis=-1, keepdims=True) + EPS) * gf_ref[...]


def _mlp(h, g_mlp, w_up, w_down, g_final, tm, tf):
    n, d = h.shape
    d_ff = w_up.shape[1]
    return pl.pallas_call(
        _mlp_kernel,
        grid=(n // tm, d_ff // tf),
        in_specs=[pl.BlockSpec((tm, d), lambda i, f: (i, 0)),
                  pl.BlockSpec((1, d), lambda i, f: (0, 0)),
                  pl.BlockSpec((d, tf), lambda i, f: (0, f)),
                  pl.BlockSpec((tf, d), lambda i, f: (f, 0)),
                  pl.BlockSpec((1, d), lambda i, f: (0, 0))],
        out_specs=pl.BlockSpec((tm, d), lambda i, f: (i, 0)),
        out_shape=jax.ShapeDtypeStruct((n, d), F32),
        scratch_shapes=[pltpu.VMEM((tm, d), BF16), pltpu.VMEM((tm, d), F32)],
        compiler_params=_cparams(("arbitrary", "arbitrary")),
        name="mlp",
    )(h, g_mlp, w_up, w_down, g_final)


def _row_tile(n, target):
    return min(n, target)


def kernel(x_prompt, x_sample, cache_diff_k, cache_diff_v, state_ret, cache_mem_k, cache_mem_v, page_table, mem_prompt, g_mix, w_in, lam_q1, lam_k1, lam_q2, lam_k2, g_subln, g_mem, w_mem_k, w_mem_v, p_a, p_b, p_c, w_o, g_mlp, w_up, w_down, g_final):
    depth = g_mix.shape[0]
    assert depth == 1
    layer = 0
    lam_init = 0.8 - 0.6 * math.exp(-0.3 * layer)
    b, t, d = x_prompt.shape
    db, dt, _ = x_sample.shape
    assert dt == 1
    n_pool, page = cache_diff_k.shape[1:3]
    past_len = page_table.shape[1] * page
    mem_len = mem_prompt.shape[1]
    n_p, n_s = b * t, db * dt

    row = lambda a: a[layer].reshape(1, -1)
    lams = (row(lam_q1), row(lam_k1), row(lam_q2), row(lam_k2))
    g_sub = row(g_subln)
    w_in_b = w_in[layer].astype(BF16)
    w_mem_b = jnp.concatenate([w_mem_k[layer], w_mem_v[layer]], axis=1).astype(BF16)
    pa_b, pb_b, pc_b, wo_b = (a[layer].astype(BF16) for a in (p_a, p_b, p_c, w_o))
    wu_b, wd_b = w_up[layer].astype(BF16), w_down[layer].astype(BF16)
    g_final2 = g_final.reshape(1, -1)

    tm_p = _row_tile(t, 512)
    xp = x_prompt.reshape(n_p, d)
    tab_p = _rope_lane_tables(jnp.arange(t))
    qt, ktf, kab, vaf, vtb, qkb, vb, gsb, qc, gates = _inproj(xp, row(g_mix), w_in_b, tab_p, tm_p, seq_len=t)
    oa = _flash_diff_attn(lams, g_sub.reshape(-1, 1), qt, kab, vtb, b, t, _row_tile(t, 512), lam_init)
    ob, state_p = _retention_prompt(qkb, vb, gsb, b, t)
    mem_f, mem_b = _rms_matmul(mem_prompt.reshape(b * mem_len, d), row(g_mem), w_mem_b,
                               _row_tile(b * mem_len, 256), 1024)
    c_w = C_HEADS * C_HD
    oc = _mem_attn_prompt(qc, mem_b[:, :c_w], mem_b[:, c_w:], t, tm_p)
    hp = _merge(xp, oa, ob, oc, gates, pa_b, pb_b, pc_b, wo_b, _row_tile(t, 256))
    y_p = _mlp(hp, row(g_mlp), wu_b, wd_b, g_final2, tm_p, 1024)

    xs = x_sample.reshape(n_s, d)
    tab_s = jnp.tile(_rope_lane_tables(past_len + jnp.arange(dt)), (db, 1))
    qa_s, kaf_s, vaf_s, qkb_s, vb_s, gsb_s, qc_s, gates_s = _inproj(xs, row(g_mix), w_in_b, tab_s, n_s)
    width_a = A_HEADS * LANES
    r3 = lambda a: a.reshape(db, 1, a.shape[-1])
    ckt = cache_diff_k[layer].transpose(0, 2, 3, 4, 1).reshape(n_pool, width_a, page)
    cv = cache_diff_v[layer].reshape(n_pool, page * A_HEADS, LANES)
    oa_s = _paged_diff_attn(page_table, lams, g_sub, r3(qa_s), r3(kaf_s), r3(vaf_s), ckt, cv, 8, lam_init)
    ob_s, state_s = _retention_sample(r3(qkb_s), r3(vb_s), r3(gsb_s), state_ret[layer])
    oc_s = _mem_attn_sample(r3(qc_s), cache_mem_k[layer].reshape(db, mem_len, c_w),
                            cache_mem_v[layer].reshape(db, mem_len, c_w))
    hs = _merge(xs, oa_s.reshape(n_s, -1), ob_s.reshape(n_s, -1), oc_s.reshape(n_s, -1), gates_s,
                pa_b, pb_b, pc_b, wo_b, n_s)
    y_s = _mlp(hs, row(g_mlp), wu_b, wd_b, g_final2, n_s, 1024)

    return (
        y_p.reshape(b, t, d),
        y_s.reshape(db, dt, d),
        ktf.reshape(1, b, A_HEADS, 2, A_HD, t).transpose(0, 1, 5, 2, 3, 4),
        vaf.reshape(1, b, t, A_HEADS, 2 * A_HD),
        state_p[None],
        mem_f[:, :c_w].reshape(1, b, mem_len, C_HEADS, C_HD),
        mem_f[:, c_w:].reshape(1, b, mem_len, C_HEADS, C_HD),
        kaf_s.reshape(1, db, dt, A_HEADS, 2, A_HD),
        vaf_s.reshape(1, db, dt, A_HEADS, 2 * A_HD),
        state_s[None],
    )
```

```python
import functools
import math

import jax
import jax.numpy as jnp
from jax import lax
from jax.experimental import pallas as pl
from jax.experimental.pallas import tpu as pltpu

F32 = jnp.float32
BF16 = jnp.bfloat16
EPS = 1e-6
NEG_BIG = -1e30
LOG2E = math.log2(math.e)

LANES = 128
VMEM_LIMIT = 56 * 1024 * 1024

A_HEADS = 8
A_HD = 64
A_ROT = A_HD // 4
ROPE_THETA = 500000.0
B_HEADS = 8
B_DK = 64
B_DV = 128
RET_THETA = 10000.0
RET_CHUNK = 128
C_HEADS = 4
C_HD = 256
SEG = 1024
N_SEG = 7

NT_DIMS = (((1,), (1,)), ((), ()))
TN_DIMS = (((0,), (0,)), ((), ()))


def _cparams(sem):
    return pltpu.CompilerParams(dimension_semantics=sem, vmem_limit_bytes=VMEM_LIMIT)


def _resident(shape, index_map):
    return pl.BlockSpec(shape, index_map, pipeline_mode=pl.Buffered(1))


def _rope_angles(pos, n_rot, theta):
    inv = 1.0 / (theta ** (jnp.arange(0, n_rot, 2, dtype=F32) / n_rot))
    ang = pos.astype(F32)[:, None] * inv[None, :]
    return jnp.cos(ang), jnp.sin(ang)


def _rope_lane_tables(pos):
    t = pos.shape[0]
    ca, sa = _rope_angles(pos, A_ROT, ROPE_THETA)
    ha = A_ROT // 2
    c_a = jnp.concatenate([ca, ca, jnp.ones((t, A_HD - A_ROT), F32)], axis=1)
    p_a = jnp.concatenate([jnp.zeros((t, ha), F32), sa, jnp.zeros((t, A_HD - A_ROT), F32)], axis=1)
    n_a = jnp.concatenate([-sa, jnp.zeros((t, A_HD - ha), F32)], axis=1)
    cb, sb = _rope_angles(pos, B_DK, RET_THETA)
    hb = B_DK // 2
    c_b = jnp.concatenate([cb, cb], axis=1)
    p_b = jnp.concatenate([jnp.zeros((t, hb), F32), sb], axis=1)
    n_b = jnp.concatenate([-sb, jnp.zeros((t, hb), F32)], axis=1)
    return jnp.concatenate([jnp.tile(a, (1, 2)) for a in (c_a, p_a, n_a, c_b, p_b, n_b)], axis=1)


def _retention_tables(c):
    log_g = jnp.log(1.0 - 2.0 ** (-5.0 - jnp.arange(B_HEADS, dtype=F32)))
    idx = jnp.arange(c, dtype=F32)
    rel = idx[:, None] - idx[None, :]
    d_in = jnp.where(rel >= 0, jnp.exp(log_g[:, None, None] * jnp.maximum(rel, 0.0)), 0.0)
    d_q = jnp.exp(log_g[:, None] * (idx + 1.0))
    d_k = jnp.exp(log_g[:, None] * (c - 1.0 - idx))
    d_c = jnp.exp(log_g * c)
    d_q = jnp.broadcast_to(d_q[:, :, None], (B_HEADS, c, LANES))
    d_k = jnp.broadcast_to(d_k[:, :, None], (B_HEADS, c, B_DK)).reshape(B_HEADS // 2, 2, c, B_DK)
    d_k = d_k.transpose(0, 2, 1, 3).reshape(B_HEADS // 2, c, 2 * B_DK)
    d_c = jnp.broadcast_to(d_c[:, None, None], (B_HEADS, B_DK, LANES)).reshape(B_HEADS // 2, 2 * B_DK, LANES)
    return d_in, d_q, d_k, d_c


def _rope_cols(r, c, p, n, half):
    outs = []
    for h in range(r.shape[1] // LANES):
        blk = r[:, h * LANES:(h + 1) * LANES]
        outs.append(blk * c + pltpu.roll(blk, half, 1) * p + pltpu.roll(blk, LANES - half, 1) * n)
    return jnp.concatenate(outs, axis=1)


def _sigmoid(x):
    return 1.0 / (1.0 + jnp.exp(-x))


def _inproj_kernel(x_ref, g_ref, w_ref, tab_ref, *refs, token_minor):
    xn_ref = refs[-1]
    if token_minor:
        qa_ref, kaf_ref, kab_ref, vaf_ref, vtb_ref, qkb_ref, vb_ref, gb_ref, qc_ref, gates_ref = refs[:-1]
    else:
        qa_ref, kaf_ref, vaf_ref, qkb_ref, vb_ref, gb_ref, qc_ref, gates_ref = refs[:-1]
    j = pl.program_id(1)

    @pl.when(j == 0)
    def _():
        x = x_ref[...]
        xn = x * lax.rsqrt(jnp.mean(x * x, axis=-1, keepdims=True) + EPS) * g_ref[...]
        xn_ref[...] = xn.astype(BF16)

    r = jnp.dot(xn_ref[...], w_ref[...], preferred_element_type=F32)

    def rope_a(v):
        return _rope_cols(v, tab_ref[:, 0:LANES], tab_ref[:, LANES:2 * LANES],
                          tab_ref[:, 2 * LANES:3 * LANES], A_ROT // 2)

    def rope_b(v):
        return _rope_cols(v, tab_ref[:, 3 * LANES:4 * LANES], tab_ref[:, 4 * LANES:5 * LANES],
                          tab_ref[:, 5 * LANES:6 * LANES], B_DK // 2)

    def finish_q(r):
        qa = rope_a(r) * (A_HD ** -0.5 * LOG2E)
        if token_minor:
            qa_ref[0] = qa.T.astype(BF16)
        else:
            qa_ref[...] = qa.astype(BF16)

    def finish_k(r):
        ka = rope_a(r)
        if token_minor:
            kaf_ref[0] = ka.T
            kab_ref[...] = ka.astype(BF16)
        else:
            kaf_ref[...] = ka

    def finish_v(r):
        vaf_ref[...] = r
        if token_minor:
            vtb_ref[0] = r.T.astype(BF16)

    def finish_qkb(r):
        rb = rope_b(r)
        half = SEG // 2
        qkb_ref[:, :half] = rb[:, :half].astype(BF16)
        qkb_ref[:, half:] = (rb[:, half:] * (B_DK ** -0.5)).astype(BF16)

    def finish_vb(r):
        vb_ref[...] = r.astype(BF16)

    def finish_gb(r):
        gb_ref[...] = (r * _sigmoid(r)).astype(BF16)

    def finish_qc(r):
        qc_ref[...] = (r * (C_HD ** -0.5)).astype(BF16)

    def finish_gates(r):
        gates_ref[...] = _sigmoid(r).astype(BF16)

    finish = (finish_q, finish_k, finish_v, finish_qkb, finish_vb, finish_gb, finish_qc)

    for k in range(N_SEG):
        @pl.when(j == k)
        def _(k=k):
            finish[k](r)

    @pl.when(j >= N_SEG)
    def _():
        finish_gates(r)


def _inproj(x, g, w, tab, tm, seq_len=None):
    n, d = x.shape
    n_col = w.shape[1] // SEG
    n_gate = n_col - N_SEG
    tab_blocks = tab.shape[0] // tm
    token_minor = seq_len is not None
    rows = lambda dt: (jax.ShapeDtypeStruct((n, SEG), dt), pl.BlockSpec((tm, SEG), lambda i, j: (i, 0)))
    if token_minor:
        per_seq = seq_len // tm
        tmin = lambda dt: (jax.ShapeDtypeStruct((n // seq_len, SEG, seq_len), dt),
                           pl.BlockSpec((1, SEG, tm), lambda i, j: (i // per_seq, 0, i % per_seq)))
        outs = [tmin(BF16), tmin(F32), rows(BF16), rows(F32), tmin(BF16)]
    else:
        outs = [rows(BF16), rows(F32), rows(F32)]
    outs += [rows(BF16)] * 4
    outs.append((jax.ShapeDtypeStruct((n, n_gate * SEG), BF16),
                 pl.BlockSpec((tm, SEG), lambda i, j: (i, jnp.maximum(j - N_SEG, 0)))))
    out_shape = [o[0] for o in outs]
    out_specs = [o[1] for o in outs]
    return pl.pallas_call(
        functools.partial(_inproj_kernel, token_minor=token_minor),
        grid=(n // tm, n_col),
        in_specs=[
            pl.BlockSpec((tm, d), lambda i, j: (i, 0)),
            pl.BlockSpec((1, d), lambda i, j: (0, 0)),
            pl.BlockSpec((d, SEG), lambda i, j: (0, j)),
            pl.BlockSpec((tm, 6 * LANES), lambda i, j: (i % tab_blocks, 0)),
        ],
        out_specs=out_specs,
        out_shape=out_shape,
        scratch_shapes=[pltpu.VMEM((tm, d), BF16)],
        compiler_params=_cparams(("arbitrary", "arbitrary")),
        name="inproj",
    )(x, g, w, tab)


def _lambda(lq1_ref, lk1_ref, lq2_ref, lk2_ref, lam_init):
    a = jnp.sum(lq1_ref[...] * lk1_ref[...], axis=-1, keepdims=True)
    b = jnp.sum(lq2_ref[...] * lk2_ref[...], axis=-1, keepdims=True)
    return jnp.exp(a) - jnp.exp(b) + lam_init


def _subln(o, g, lam_init):
    y = o * lax.rsqrt(jnp.mean(o * o, axis=-1, keepdims=True) + EPS) * g
    return y * (1.0 - lam_init)


def _flash_kernel(lq1_ref, lk1_ref, lq2_ref, lk2_ref, gs_ref, qt_ref, k_ref, vt_ref, o_ref,
                  sa_ref, sb_ref, m_ref, l_ref, acc_ref, *, tq, lam_init):
    qi = pl.program_id(2)
    qt = qt_ref[0]
    feat = lax.broadcasted_iota(jnp.int32, qt.shape, 0)
    zero = jnp.zeros_like(qt)
    qs = jnp.concatenate([jnp.where(feat < A_HD, qt, zero), jnp.where(feat >= A_HD, qt, zero)], axis=1)

    s_refs = (sa_ref, sb_ref)

    def qk(kj, slot):
        start = pl.multiple_of(kj * tq, tq)
        s_refs[slot][...] = jnp.dot(k_ref[pl.ds(start, tq), :], qs, preferred_element_type=F32)

    def upd(kj, slot, diagonal):
        start = pl.multiple_of(kj * tq, tq)
        vb = vt_ref[0, :, pl.ds(start, tq)]
        s = s_refs[slot][...]
        if diagonal:
            key = lax.broadcasted_iota(jnp.int32, s.shape, 0)
            qry = lax.broadcasted_iota(jnp.int32, s.shape, 1) & (tq - 1)
            s = jnp.where(key <= qry, s, NEG_BIG)
        m = m_ref[...]
        m_new = jnp.maximum(m, jnp.max(s, axis=0, keepdims=True))
        corr = jnp.exp2(m - m_new)
        p = jnp.exp2(s - m_new)
        m_ref[...] = m_new
        l_ref[...] = l_ref[...] * corr + jnp.sum(p, axis=0, keepdims=True)
        acc_ref[...] = acc_ref[...] * corr + jnp.dot(vb, p.astype(BF16), preferred_element_type=F32)

    m_ref[...] = jnp.full(m_ref.shape, NEG_BIG, F32)
    l_ref[...] = jnp.zeros(l_ref.shape, F32)
    acc_ref[...] = jnp.zeros(acc_ref.shape, F32)
    qk(0, 0)

    def pair(pi, carry):
        b0 = 2 * pi
        qk(b0 + 1, 1)
        upd(b0, 0, False)
        qk(b0 + 2, 0)
        upd(b0 + 1, 1, False)
        return carry

    lax.fori_loop(0, qi // 2, pair, 0)

    @pl.when(qi % 2 == 1)
    def _():
        qk(qi, 1)
        upd(qi - 1, 0, False)
        upd(qi, 1, True)

    @pl.when(qi % 2 == 0)
    def _():
        upd(qi, 0, True)

    l = l_ref[...]
    acc = acc_ref[...]
    lam = _lambda(lq1_ref, lk1_ref, lq2_ref, lk2_ref, lam_init)
    o = acc[:, :tq] / l[:, :tq] - lam * (acc[:, tq:] / l[:, tq:])
    y = o * lax.rsqrt(jnp.mean(o * o, axis=0, keepdims=True) + EPS) * gs_ref[...] * (1.0 - lam_init)
    o_ref[...] = y.T.astype(BF16)


def _flash_diff_attn(lams, g_sub_col, qt, k, vt, b, t, tq, lam_init):
    assert tq & (tq - 1) == 0 and t % tq == 0
    nq = t // tq
    n = b * t
    vec = lambda shape: pl.BlockSpec(shape, lambda bi, h, qi: (0, 0))
    return pl.pallas_call(
        functools.partial(_flash_kernel, tq=tq, lam_init=lam_init),
        grid=(b, A_HEADS, nq),
        in_specs=[vec((1, A_HD))] * 4 + [
            vec((LANES, 1)),
            pl.BlockSpec((1, LANES, tq), lambda bi, h, qi: (bi, h, qi)),
            pl.BlockSpec((t, LANES), lambda bi, h, qi: (bi, h)),
            pl.BlockSpec((1, LANES, t), lambda bi, h, qi: (bi, h, 0)),
        ],
        out_specs=pl.BlockSpec((tq, LANES), lambda bi, h, qi: (bi * nq + qi, h)),
        out_shape=jax.ShapeDtypeStruct((n, A_HEADS * LANES), BF16),
        scratch_shapes=[pltpu.VMEM((tq, 2 * tq), F32), pltpu.VMEM((tq, 2 * tq), F32),
                        pltpu.VMEM((1, 2 * tq), F32), pltpu.VMEM((1, 2 * tq), F32),
                        pltpu.VMEM((LANES, 2 * tq), F32)],
        compiler_params=_cparams(("arbitrary", "arbitrary", "arbitrary")),
        name="flash_diff_attn",
    )(*lams, g_sub_col, qt, k, vt)


PAGED_ROWS = 2 * A_HEADS
PAGED_WIDTH = A_HEADS * LANES


def _paged_start(q_ref, kn_ref, vn_ref, qd_ref, m_ref, l_ref, acc_ref):
    rows, width = PAGED_ROWS, PAGED_WIDTH
    q = jnp.broadcast_to(q_ref[0].astype(F32), (rows, width))
    row = lax.broadcasted_iota(jnp.int32, (rows, width), 0)
    lane = lax.broadcasted_iota(jnp.int32, (rows, width), 1)
    qd = jnp.where(lane // A_HD == row, q, 0.0)
    qd_ref[...] = qd.astype(BF16)
    s_self = jnp.sum(qd * kn_ref[0], axis=-1, keepdims=True)
    m_ref[...] = jnp.broadcast_to(s_self, (rows, LANES))
    l_ref[...] = jnp.ones((rows, LANES), F32)
    acc_ref[...] = jnp.broadcast_to(vn_ref[0], (rows, width))


def _paged_scores(k_refs, qd_ref):
    qd = qd_ref[...]
    return jnp.concatenate([jnp.dot(qd, k_ref[0].astype(BF16), preferred_element_type=F32)
                            for k_ref in k_refs], axis=1)


def _paged_softmax(s, m_ref, l_ref):
    rows = PAGED_ROWS
    m = m_ref[:, 0:1]
    m_new = jnp.maximum(m, jnp.max(s, axis=-1, keepdims=True))
    corr = jnp.exp2(m - m_new)
    p = jnp.exp2(s - m_new)
    l = l_ref[:, 0:1] * corr + jnp.sum(p, axis=-1, keepdims=True)
    m_ref[...] = jnp.broadcast_to(m_new, (rows, LANES))
    l_ref[...] = jnp.broadcast_to(l, (rows, LANES))
    return p.astype(BF16), corr


def _paged_values(v_refs, pb, heads):
    page = v_refs[0].shape[1] // A_HEADS
    outs = []
    for h in heads:
        vh = jnp.concatenate([v_ref[0, pl.ds(h, page, stride=A_HEADS), :].astype(BF16)
                              for v_ref in v_refs], axis=0)
        outs.append(jnp.dot(pb, vh, preferred_element_type=F32))
    return outs


def _paged_accumulate(acc_ref, corr, outs, heads):
    for h, o in zip(heads, outs):
        cols = slice(h * LANES, (h + 1) * LANES)
        acc_ref[:, cols] = acc_ref[:, cols] * corr + o


def _paged_pages(k_refs, v_refs, qd_ref, m_ref, l_ref, acc_ref):
    pb, corr = _paged_softmax(_paged_scores(k_refs, qd_ref), m_ref, l_ref)
    heads = range(A_HEADS)
    _paged_accumulate(acc_ref, corr, _paged_values(v_refs, pb, heads), heads)


def _paged_finish(lam_refs, gs_ref, l_ref, acc_ref, o_ref, lam_init):
    lam = _lambda(*lam_refs, lam_init)
    g = gs_ref[...]
    outs = []
    for h in range(A_HEADS):
        cols = slice(h * LANES, (h + 1) * LANES)
        a0 = acc_ref[2 * h:2 * h + 1, cols] / l_ref[2 * h:2 * h + 1, 0:1]
        a1 = acc_ref[2 * h + 1:2 * h + 2, cols] / l_ref[2 * h + 1:2 * h + 2, 0:1]
        outs.append(_subln(a0 - lam * a1, g, lam_init))
    o_ref[0] = jnp.concatenate(outs, axis=1).astype(BF16)


def _paged_kernel(pt_ref, lq1_ref, lk1_ref, lq2_ref, lk2_ref, gs_ref, q_ref, kn_ref, vn_ref, *rest,
                  pages, lam_init):
    k_refs = rest[:pages]
    v_refs = rest[pages:2 * pages]
    o_ref = rest[2 * pages]
    state = rest[2 * pages + 1:]
    j = pl.program_id(1)

    @pl.when(j == 0)
    def _():
        _paged_start(q_ref, kn_ref, vn_ref, *state)

    _paged_pages(k_refs, v_refs, *state)

    @pl.when(j == pl.num_programs(1) - 1)
    def _():
        _paged_finish((lq1_ref, lk1_ref, lq2_ref, lk2_ref), gs_ref, state[2], state[3], o_ref, lam_init)


def _paged_specs(pages, width, page, seq_of, group_of):
    vec = lambda shape: pl.BlockSpec(shape, lambda a, b, pt: (0, 0))
    row3 = pl.BlockSpec((1, 1, width), lambda a, b, pt: (seq_of(a, b), 0, 0))

    def page_spec(r, shape):
        return pl.BlockSpec((1,) + shape,
                            lambda a, b, pt: (pt[seq_of(a, b), group_of(a, b) * pages + r], 0, 0))

    in_specs = ([vec((1, A_HD))] * 4 + [vec((1, LANES)), row3, row3, row3]
                + [page_spec(r, (width, page)) for r in range(pages)]
                + [page_spec(r, (page * A_HEADS, LANES)) for r in range(pages)])
    scratch = [pltpu.VMEM((PAGED_ROWS, width), BF16), pltpu.VMEM((PAGED_ROWS, LANES), F32),
               pltpu.VMEM((PAGED_ROWS, LANES), F32), pltpu.VMEM((PAGED_ROWS, width), F32)]
    return in_specs, row3, scratch


def _paged_diff_attn(page_table, lams, g_sub, q, k_new, v_new, cache_kt, cache_v, pages, lam_init):
    db, n_pages = page_table.shape
    width, page = cache_kt.shape[1:]
    assert n_pages % pages == 0 and cache_v.shape[1:] == (page * A_HEADS, LANES) and width == PAGED_WIDTH
    in_specs, out_spec, scratch = _paged_specs(pages, width, page, lambda a, b: a, lambda a, b: b)
    grid_spec = pltpu.PrefetchScalarGridSpec(
        num_scalar_prefetch=1, grid=(db, n_pages // pages),
        in_specs=in_specs, out_specs=out_spec, scratch_shapes=scratch)
    return pl.pallas_call(
        functools.partial(_paged_kernel, pages=pages, lam_init=lam_init),
        grid_spec=grid_spec,
        out_shape=jax.ShapeDtypeStruct((db, 1, width), BF16),
        compiler_params=_cparams(("arbitrary", "arbitrary")),
        name="paged_diff_attn",
    )(page_table, *lams, g_sub, q, k_new, v_new, *([cache_kt] * pages), *([cache_v] * pages))


def _head_mask(x, e):
    lane = lax.broadcasted_iota(jnp.int32, x.shape, x.ndim - 1)
    keep = (lane >= B_DK) if e else (lane < B_DK)
    return jnp.where(keep, x, jnp.zeros_like(x))


def _rms_plain(o):
    return o * lax.rsqrt(jnp.mean(o * o, axis=-1, keepdims=True) + EPS)


def _retention_kernel(qk_ref, v_ref, gs_ref, din_ref, dq_ref, dk_ref, dc_ref, o_ref, s_out_ref, s_ref):
    ci = pl.program_id(1)

    @pl.when(ci == 0)
    def _():
        s_ref[...] = jnp.zeros_like(s_ref)

    half = B_HEADS * B_DK
    for p in range(B_HEADS // 2):
        q2 = qk_ref[:, p * LANES:(p + 1) * LANES]
        k2 = qk_ref[:, half + p * LANES:half + (p + 1) * LANES]
        s_old = s_ref[p]
        s_bf = s_old.astype(BF16)
        kd = (k2.astype(F32) * dk_ref[p]).astype(BF16)
        new_rows = []
        for e in range(2):
            h = 2 * p + e
            qz = _head_mask(q2, e)
            vh = v_ref[:, h * LANES:(h + 1) * LANES]
            attn = lax.dot_general(qz, k2, NT_DIMS, preferred_element_type=F32) * din_ref[h]
            o = (jnp.dot(attn.astype(BF16), vh, preferred_element_type=F32)
                 + jnp.dot(qz, s_bf, preferred_element_type=F32) * dq_ref[h])
            u = lax.dot_general(kd, vh, TN_DIMS, preferred_element_type=F32)
            rows = slice(e * B_DK, (e + 1) * B_DK)
            new_rows.append(s_old[rows] * dc_ref[p][rows] + u[rows])
            gate = gs_ref[:, h * LANES:(h + 1) * LANES].astype(F32)
            o_ref[:, h * LANES:(h + 1) * LANES] = (_rms_plain(o) * gate).astype(BF16)
        s_ref[p] = jnp.concatenate(new_rows, axis=0)

    @pl.when(ci == pl.num_programs(1) - 1)
    def _():
        s_out_ref[0] = s_ref[...].reshape(B_HEADS, B_DK, B_DV)


def _retention_prompt(qk, v, gs, b, t):
    c = math.gcd(t, RET_CHUNK)
    nc = t // c
    n = b * t
    d_in, d_q, d_k, d_c = _retention_tables(c)
    rows = lambda bi, ci: (bi * nc + ci, 0)
    const3 = lambda bi, ci: (0, 0, 0)
    width = B_HEADS * B_DV
    return pl.pallas_call(
        _retention_kernel,
        grid=(b, nc),
        in_specs=[
            pl.BlockSpec((c, 2 * B_HEADS * B_DK), rows),
            pl.BlockSpec((c, width), rows),
            pl.BlockSpec((c, width), rows),
            pl.BlockSpec(d_in.shape, const3),
            pl.BlockSpec(d_q.shape, const3),
            pl.BlockSpec(d_k.shape, const3),
            pl.BlockSpec(d_c.shape, const3),
        ],
        out_specs=[
            pl.BlockSpec((c, width), rows),
            pl.BlockSpec((1, B_HEADS, B_DK, B_DV), lambda bi, ci: (bi, 0, 0, 0)),
        ],
        out_shape=[jax.ShapeDtypeStruct((n, width), BF16),
                   jax.ShapeDtypeStruct((b, B_HEADS, B_DK, B_DV), F32)],
        scratch_shapes=[pltpu.VMEM((B_HEADS // 2, 2 * B_DK, B_DV), F32)],
        compiler_params=_cparams(("arbitrary", "arbitrary")),
        name="retention_prompt",
    )(qk, v, gs, d_in, d_q, d_k, d_c)


def _retention_step_kernel(qk_ref, v_ref, gs_ref, s0_ref, dq_ref, dc_ref, o_ref, s_out_ref):
    half = B_HEADS * B_DK
    pad = 16
    row0 = lax.broadcasted_iota(jnp.int32, (pad, LANES), 0) == 0
    qk = qk_ref[0].astype(F32)
    v = v_ref[0].astype(F32)

    def pad_rows(x):
        return jnp.where(row0, jnp.broadcast_to(x, (pad, LANES)), 0.0).astype(BF16)

    for p in range(B_HEADS // 2):
        q2 = qk[:, p * LANES:(p + 1) * LANES]
        k2 = qk[:, half + p * LANES:half + (p + 1) * LANES]
        s_old = s0_ref[0, 2 * p:2 * p + 2].reshape(2 * B_DK, B_DV)
        s_bf = s_old.astype(BF16)
        k8 = pad_rows(k2)
        new_rows = []
        for e in range(2):
            h = 2 * p + e
            qz = _head_mask(q2, e)
            vh = v[:, h * LANES:(h + 1) * LANES]
            qk_dot = jnp.sum(qz * k2, axis=-1, keepdims=True)
            qs = jnp.dot(pad_rows(qz), s_bf, preferred_element_type=F32)[0:1]
            o = qk_dot * vh + qs * dq_ref[h][0:1]
            u = lax.dot_general(k8, pad_rows(vh), TN_DIMS, preferred_element_type=F32)
            rows = slice(e * B_DK, (e + 1) * B_DK)
            new_rows.append(s_old[rows] * dc_ref[p][rows] + u[rows])
            gate = gs_ref[0][:, h * LANES:(h + 1) * LANES].astype(F32)
            o_ref[0, :, h * LANES:(h + 1) * LANES] = (_rms_plain(o) * gate).astype(BF16)
        s_out_ref[0, 2 * p:2 * p + 2] = jnp.concatenate(new_rows, axis=0).reshape(2, B_DK, B_DV)


def _retention_sample(qk, v, gs, s0):
    db = qk.shape[0]
    _, d_q, _, d_c = _retention_tables(1)
    width = B_HEADS * B_DV
    row3 = lambda w: pl.BlockSpec((1, 1, w), lambda bi: (bi, 0, 0))
    state = pl.BlockSpec((1, B_HEADS, B_DK, B_DV), lambda bi: (bi, 0, 0, 0))
    const3 = lambda bi: (0, 0, 0)
    return pl.pallas_call(
        _retention_step_kernel,
        grid=(db,),
        in_specs=[row3(2 * B_HEADS * B_DK), row3(width), row3(width), state,
                  pl.BlockSpec(d_q.shape, const3), pl.BlockSpec(d_c.shape, const3)],
        out_specs=[row3(width), state],
        out_shape=[jax.ShapeDtypeStruct((db, 1, width), BF16),
                   jax.ShapeDtypeStruct(s0.shape, F32)],
        compiler_params=_cparams(("arbitrary",)),
        name="retention_sample",
    )(qk, v, gs, s0, d_q, d_c)


def _rms_matmul_kernel(x_ref, g_ref, w_ref, of_ref, ob_ref, xn_ref):
    @pl.when(pl.program_id(1) == 0)
    def _():
        x = x_ref[...]
        xn = x * lax.rsqrt(jnp.mean(x * x, axis=-1, keepdims=True) + EPS) * g_ref[...]
        xn_ref[...] = xn.astype(BF16)

    r = jnp.dot(xn_ref[...], w_ref[...], preferred_element_type=F32)
    of_ref[...] = r
    ob_ref[...] = r.astype(BF16)


def _rms_matmul(x, g, w, tm, tn):
    n, d = x.shape
    width = w.shape[1]
    return pl.pallas_call(
        _rms_matmul_kernel,
        grid=(n // tm, width // tn),
        in_specs=[pl.BlockSpec((tm, d), lambda i, j: (i, 0)),
                  pl.BlockSpec((1, d), lambda i, j: (0, 0)),
                  pl.BlockSpec((d, tn), lambda i, j: (0, j))],
        out_specs=[pl.BlockSpec((tm, tn), lambda i, j: (i, j))] * 2,
        out_shape=[jax.ShapeDtypeStruct((n, width), F32), jax.ShapeDtypeStruct((n, width), BF16)],
        scratch_shapes=[pltpu.VMEM((tm, d), BF16)],
        compiler_params=_cparams(("arbitrary", "arbitrary")),
        name="mem_kv",
    )(x, g, w)


def _softmax_rows(s):
    m = jnp.max(s, axis=-1, keepdims=True)
    p = jnp.exp(s - m)
    return p / jnp.sum(p, axis=-1, keepdims=True)


def _mem_attn_kernel(q_ref, mk_ref, mv_ref, o_ref):
    for h in range(C_HEADS):
        cols = slice(h * C_HD, (h + 1) * C_HD)
        s = lax.dot_general(q_ref[:, cols], mk_ref[:, cols], NT_DIMS, preferred_element_type=F32)
        p = _softmax_rows(s)
        o_ref[:, cols] = jnp.dot(p.astype(BF16), mv_ref[:, cols], preferred_element_type=F32).astype(BF16)


def _mem_attn_prompt(q, mk, mv, t, tm):
    n, width = q.shape
    mem_len = mk.shape[0] // (n // t)
    per_seq = t // tm
    return pl.pallas_call(
        _mem_attn_kernel,
        grid=(n // tm,),
        in_specs=[pl.BlockSpec((tm, width), lambda i: (i, 0)),
                  pl.BlockSpec((mem_len, width), lambda i: (i // per_seq, 0)),
                  pl.BlockSpec((mem_len, width), lambda i: (i // per_seq, 0))],
        out_specs=pl.BlockSpec((tm, width), lambda i: (i, 0)),
        out_shape=jax.ShapeDtypeStruct((n, width), BF16),
        compiler_params=_cparams(("arbitrary",)),
        name="mem_attn_prompt",
    )(q, mk, mv)


def _mem_attn_step_kernel(q_ref, mk_ref, mv_ref, o_ref):
    pad = 16
    width = C_HEADS * C_HD
    q = jnp.broadcast_to(q_ref[0].astype(F32), (pad, width))
    row = lax.broadcasted_iota(jnp.int32, (pad, width), 0)
    lane = lax.broadcasted_iota(jnp.int32, (pad, width), 1)
    qd = jnp.where(lane // C_HD == row, q, 0.0).astype(BF16)
    s = lax.dot_general(qd, mk_ref[0].astype(BF16), NT_DIMS, preferred_element_type=F32)
    p = _softmax_rows(s)
    o = jnp.dot(p.astype(BF16), mv_ref[0].astype(BF16), preferred_element_type=F32)
    o_ref[0] = jnp.concatenate([o[h:h + 1, h * C_HD:(h + 1) * C_HD] for h in range(C_HEADS)],
                               axis=1).astype(BF16)


def _mem_attn_sample(q, mk, mv):
    db, mem_len, width = mk.shape
    row3 = pl.BlockSpec((1, 1, width), lambda bi: (bi, 0, 0))
    mem = pl.BlockSpec((1, mem_len, width), lambda bi: (bi, 0, 0))
    return pl.pallas_call(
        _mem_attn_step_kernel,
        grid=(db,),
        in_specs=[row3, mem, mem],
        out_specs=row3,
        out_shape=jax.ShapeDtypeStruct((db, 1, width), BF16),
        compiler_params=_cparams(("arbitrary",)),
        name="mem_attn_sample",
    )(q, mk, mv)


def _merge_kernel(x_ref, oa_ref, ob_ref, oc_ref, gates_ref, pa_ref, pb_ref, pc_ref, wo_ref, h_ref):
    d = x_ref.shape[1]
    merged = (gates_ref[:, 0:d].astype(F32) * jnp.dot(oa_ref[...], pa_ref[...], preferred_element_type=F32)
              + gates_ref[:, d:2 * d].astype(F32) * jnp.dot(ob_ref[...], pb_ref[...], preferred_element_type=F32)
              + gates_ref[:, 2 * d:3 * d].astype(F32) * jnp.dot(oc_ref[...], pc_ref[...], preferred_element_type=F32))
    h_ref[...] = x_ref[...] + jnp.dot(merged.astype(BF16), wo_ref[...], preferred_element_type=F32)


def _merge(x, oa, ob, oc, gates, p_a, p_b, p_c, w_o, tm):
    n, d = x.shape
    rows = lambda w: pl.BlockSpec((tm, w), lambda i: (i, 0))
    whole = lambda a: _resident(a.shape, lambda i: (0, 0))
    return pl.pallas_call(
        _merge_kernel,
        grid=(n // tm,),
        in_specs=[rows(d), rows(oa.shape[1]), rows(ob.shape[1]), rows(oc.shape[1]), rows(gates.shape[1]),
                  whole(p_a), whole(p_b), whole(p_c), whole(w_o)],
        out_specs=rows(d),
        out_shape=jax.ShapeDtypeStruct((n, d), F32),
        compiler_params=_cparams(("arbitrary",)),
        name="merge",
    )(x, oa, ob, oc, gates, p_a, p_b, p_c, w_o)


def _mlp_kernel(h_ref, g_ref, wu_ref, wd_ref, gf_ref, y_ref, hn_ref, acc_ref):
    f = pl.program_id(1)

    @pl.when(f == 0)
    def _():
        h = h_ref[...]
        hn = h * lax.rsqrt(jnp.mean(h * h, axis=-1, keepdims=True) + EPS) * g_ref[...]
        hn_ref[...] = hn.astype(BF16)
        acc_ref[...] = jnp.zeros_like(acc_ref)

    u = jnp.dot(hn_ref[...], wu_ref[...], preferred_element_type=F32)
    a = jnp.square(jnp.maximum(u, 0.0)).astype(BF16)
    acc_ref[...] += jnp.dot(a, wd_ref[...], preferred_element_type=F32)

    @pl.when(f == pl.num_programs(1) - 1)
    def _():
        h2 = h_ref[...] + acc_ref[...]
        y_ref[...] = h2 * lax.rsqrt(jnp.mean(h2 * h2, axis=-1, keepdims=True) + EPS) * gf_ref[...]


def _mlp_paged_kernel(pt_ref, h_ref, g_ref, wu_ref, wd_ref, gf_ref, lq1_ref, lk1_ref, lq2_ref, lk2_ref,
                      gs_ref, q_ref, kn_ref, vn_ref, *rest, pages, lam_init):
    k_refs = rest[:pages]
    v_refs = rest[pages:2 * pages]
    y_ref, o_ref, hn_ref, acc_ref = rest[2 * pages:2 * pages + 4]
    state = rest[2 * pages + 4:]
    f = pl.program_id(1)
    last = pl.num_programs(1) - 1

    @pl.when(f == 0)
    def _():
        h = h_ref[...]
        hn = h * lax.rsqrt(jnp.mean(h * h, axis=-1, keepdims=True) + EPS) * g_ref[...]
        hn_ref[...] = hn.astype(BF16)
        acc_ref[...] = jnp.zeros_like(acc_ref)
        _paged_start(q_ref, kn_ref, vn_ref, *state)

    qd_ref, pm_ref, pl_ref, pacc_ref = state
    s = _paged_scores(k_refs, qd_ref)
    u = jnp.dot(hn_ref[...], wu_ref[...], preferred_element_type=F32)
    pb, corr = _paged_softmax(s, pm_ref, pl_ref)
    first, second = range(A_HEADS // 2), range(A_HEADS // 2, A_HEADS)
    o_first = _paged_values(v_refs, pb, first)
    a = jnp.square(jnp.maximum(u, 0.0)).astype(BF16)
    o_second = _paged_values(v_refs, pb, second)
    acc_ref[...] += jnp.dot(a, wd_ref[...], preferred_element_type=F32)
    _paged_accumulate(pacc_ref, corr, o_first, first)
    _paged_accumulate(pacc_ref, corr, o_second, second)

    @pl.when(f == last)
    def _():
        h2 = h_ref[...] + acc_ref[...]
        y_ref[...] = h2 * lax.rsqrt(jnp.mean(h2 * h2, axis=-1, keepdims=True) + EPS) * gf_ref[...]
        _paged_finish((lq1_ref, lk1_ref, lq2_ref, lk2_ref), gs_ref, state[2], state[3], o_ref, lam_init)


def _mlp_with_paged(h, g_mlp, w_up, w_down, g_final, tm, tf,
                    page_table, lams, g_sub, q, k_new, v_new, cache_kt, cache_v, lam_init):
    n, d = h.shape
    d_ff = w_up.shape[1]
    db, n_pages = page_table.shape
    n_f = d_ff // tf
    width, page = cache_kt.shape[1:]
    assert n // tm == db and n_pages % n_f == 0
    pages = n_pages // n_f
    paged_in, paged_out, paged_scratch = _paged_specs(pages, width, page, lambda i, f: i, lambda i, f: f)
    grid_spec = pltpu.PrefetchScalarGridSpec(
        num_scalar_prefetch=1,
        grid=(n // tm, n_f),
        in_specs=[pl.BlockSpec((tm, d), lambda i, f, pt: (i, 0)),
                  pl.BlockSpec((1, d), lambda i, f, pt: (0, 0)),
                  pl.BlockSpec((d, tf), lambda i, f, pt: (0, f)),
                  pl.BlockSpec((tf, d), lambda i, f, pt: (f, 0)),
                  pl.BlockSpec((1, d), lambda i, f, pt: (0, 0))] + paged_in,
        out_specs=[pl.BlockSpec((tm, d), lambda i, f, pt: (i, 0)), paged_out],
        scratch_shapes=[pltpu.VMEM((tm, d), BF16), pltpu.VMEM((tm, d), F32)] + paged_scratch,
    )
    return pl.pallas_call(
        functools.partial(_mlp_paged_kernel, pages=pages, lam_init=lam_init),
        grid_spec=grid_spec,
        out_shape=[jax.ShapeDtypeStruct((n, d), F32), jax.ShapeDtypeStruct((db, 1, width), BF16)],
        compiler_params=_cparams(("arbitrary", "arbitrary")),
        name="mlp_paged",
    )(page_table, h, g_mlp, w_up, w_down, g_final, *lams, g_sub, q, k_new, v_new,
      *([cache_kt] * pages), *([cache_v] * pages))


def _mlp(h, g_mlp, w_up, w_down, g_final, tm, tf):
    n, d = h.shape
    d_ff = w_up.shape[1]
    return pl.pallas_call(
        _mlp_kernel,
        grid=(n // tm, d_ff // tf),
        in_specs=[pl.BlockSpec((tm, d), lambda i, f: (i, 0)),
                  pl.BlockSpec((1, d), lambda i, f: (0, 0)),
                  pl.BlockSpec((d, tf), lambda i, f: (0, f)),
                  pl.BlockSpec((tf, d), lambda i, f: (f, 0)),
                  pl.BlockSpec((1, d), lambda i, f: (0, 0))],
        out_specs=pl.BlockSpec((tm, d), lambda i, f: (i, 0)),
        out_shape=jax.ShapeDtypeStruct((n, d), F32),
        scratch_shapes=[pltpu.VMEM((tm, d), BF16), pltpu.VMEM((tm, d), F32)],
        compiler_params=_cparams(("arbitrary", "arbitrary")),
        name="mlp",
    )(h, g_mlp, w_up, w_down, g_final)


def _row_tile(n, target):
    return min(n, target)


def kernel(x_prompt, x_sample, cache_diff_k, cache_diff_v, state_ret, cache_mem_k, cache_mem_v, page_table, mem_prompt, g_mix, w_in, lam_q1, lam_k1, lam_q2, lam_k2, g_subln, g_mem, w_mem_k, w_mem_v, p_a, p_b, p_c, w_o, g_mlp, w_up, w_down, g_final):
    depth = g_mix.shape[0]
    assert depth == 1
    layer = 0
    lam_init = 0.8 - 0.6 * math.exp(-0.3 * layer)
    b, t, d = x_prompt.shape
    db, dt, _ = x_sample.shape
    assert dt == 1
    n_pool, page = cache_diff_k.shape[1:3]
    past_len = page_table.shape[1] * page
    mem_len = mem_prompt.shape[1]
    n_p, n_s = b * t, db * dt

    row = lambda a: a[layer].reshape(1, -1)
    lams = (row(lam_q1), row(lam_k1), row(lam_q2), row(lam_k2))
    g_sub = row(g_subln)
    w_in_b = w_in[layer].astype(BF16)
    w_mem_b = jnp.concatenate([w_mem_k[layer], w_mem_v[layer]], axis=1).astype(BF16)
    pa_b, pb_b, pc_b, wo_b = (a[layer].astype(BF16) for a in (p_a, p_b, p_c, w_o))
    wu_b, wd_b = w_up[layer].astype(BF16), w_down[layer].astype(BF16)
    g_final2 = g_final.reshape(1, -1)

    tm_p = _row_tile(t, 512)
    xp = x_prompt.reshape(n_p, d)
    tab_p = _rope_lane_tables(jnp.arange(t))
    qt, ktf, kab, vaf, vtb, qkb, vb, gsb, qc, gates = _inproj(xp, row(g_mix), w_in_b, tab_p, tm_p, seq_len=t)
    oa = _flash_diff_attn(lams, g_sub.reshape(-1, 1), qt, kab, vtb, b, t, _row_tile(t, 512), lam_init)
    ob, state_p = _retention_prompt(qkb, vb, gsb, b, t)
    mem_f, mem_b = _rms_matmul(mem_prompt.reshape(b * mem_len, d), row(g_mem), w_mem_b,
                               _row_tile(b * mem_len, 256), 1024)
    c_w = C_HEADS * C_HD
    oc = _mem_attn_prompt(qc, mem_b[:, :c_w], mem_b[:, c_w:], t, tm_p)
    hp = _merge(xp, oa, ob, oc, gates, pa_b, pb_b, pc_b, wo_b, _row_tile(t, 256))

    xs = x_sample.reshape(n_s, d)
    tab_s = jnp.tile(_rope_lane_tables(past_len + jnp.arange(dt)), (db, 1))
    qa_s, kaf_s, vaf_s, qkb_s, vb_s, gsb_s, qc_s, gates_s = _inproj(xs, row(g_mix), w_in_b, tab_s, n_s)
    width_a = A_HEADS * LANES
    r3 = lambda a: a.reshape(db, 1, a.shape[-1])
    ckt = cache_diff_k[layer].transpose(0, 2, 3, 4, 1).reshape(n_pool, width_a, page)
    cv = cache_diff_v[layer].reshape(n_pool, page * A_HEADS, LANES)
    paged_args = (page_table, lams, g_sub, r3(qa_s), r3(kaf_s), r3(vaf_s), ckt, cv)
    tf_fused = 512
    d_ff = wu_b.shape[1]
    if n_p // tm_p == db and page_table.shape[1] % (d_ff // tf_fused) == 0:
        y_p, oa_s = _mlp_with_paged(hp, row(g_mlp), wu_b, wd_b, g_final2, tm_p, tf_fused, *paged_args, lam_init)
    else:
        y_p = _mlp(hp, row(g_mlp), wu_b, wd_b, g_final2, tm_p, 1024)
        oa_s = _paged_diff_attn(*paged_args, 8, lam_init)
    ob_s, state_s = _retention_sample(r3(qkb_s), r3(vb_s), r3(gsb_s), state_ret[layer])
    oc_s = _mem_attn_sample(r3(qc_s), cache_mem_k[layer].reshape(db, mem_len, c_w),
                            cache_mem_v[layer].reshape(db, mem_len, c_w))
    hs = _merge(xs, oa_s.reshape(n_s, -1), ob_s.reshape(n_s, -1), oc_s.reshape(n_s, -1), gates_s,
                pa_b, pb_b, pc_b, wo_b, n_s)
    y_s = _mlp(hs, row(g_mlp), wu_b, wd_b, g_final2, n_s, 1024)

    return (
        y_p.reshape(b, t, d),
        y_s.reshape(db, dt, d),
        ktf.reshape(1, b, A_HEADS, 2, A_HD, t).transpose(0, 1, 5, 2, 3, 4),
        vaf.reshape(1, b, t, A_HEADS, 2 * A_HD),
        state_p[None],
        mem_f[:, :c_w].reshape(1, b, mem_len, C_HEADS, C_HD),
        mem_f[:, c_w:].reshape(1, b, mem_len, C_HEADS, C_HD),
        kaf_s.reshape(1, db, dt, A_HEADS, 2, A_HD),
        vaf_s.reshape(1, db, dt, A_HEADS, 2 * A_HD),
        state_s[None],
    )
```

```python
import functools
import math

import jax
import jax.numpy as jnp
from jax import lax
from jax.experimental import pallas as pl
from jax.experimental.pallas import tpu as pltpu

F32 = jnp.float32
BF16 = jnp.bfloat16
EPS = 1e-6
NEG_BIG = -1e30
LOG2E = math.log2(math.e)

LANES = 128
VMEM_LIMIT = 56 * 1024 * 1024

A_HEADS = 8
A_HD = 64
A_ROT = A_HD // 4
ROPE_THETA = 500000.0
B_HEADS = 8
B_DK = 64
B_DV = 128
RET_THETA = 10000.0
RET_CHUNK = 128
RET_CHUNKS_PER_STEP = 4
C_HEADS = 4
C_HD = 256
SEG = 1024
N_SEG = 7

NT_DIMS = (((1,), (1,)), ((), ()))
TN_DIMS = (((0,), (0,)), ((), ()))


def _cparams(sem):
    return pltpu.CompilerParams(dimension_semantics=sem, vmem_limit_bytes=VMEM_LIMIT)


def _resident(shape, index_map):
    return pl.BlockSpec(shape, index_map, pipeline_mode=pl.Buffered(1))


def _rope_angles(pos, n_rot, theta):
    inv = 1.0 / (theta ** (jnp.arange(0, n_rot, 2, dtype=F32) / n_rot))
    ang = pos.astype(F32)[:, None] * inv[None, :]
    return jnp.cos(ang), jnp.sin(ang)


def _rope_lane_tables(pos):
    t = pos.shape[0]
    ca, sa = _rope_angles(pos, A_ROT, ROPE_THETA)
    ha = A_ROT // 2
    c_a = jnp.concatenate([ca, ca, jnp.ones((t, A_HD - A_ROT), F32)], axis=1)
    p_a = jnp.concatenate([jnp.zeros((t, ha), F32), sa, jnp.zeros((t, A_HD - A_ROT), F32)], axis=1)
    n_a = jnp.concatenate([-sa, jnp.zeros((t, A_HD - ha), F32)], axis=1)
    cb, sb = _rope_angles(pos, B_DK, RET_THETA)
    hb = B_DK // 2
    c_b = jnp.concatenate([cb, cb], axis=1)
    p_b = jnp.concatenate([jnp.zeros((t, hb), F32), sb], axis=1)
    n_b = jnp.concatenate([-sb, jnp.zeros((t, hb), F32)], axis=1)
    return jnp.concatenate([jnp.tile(a, (1, 2)) for a in (c_a, p_a, n_a, c_b, p_b, n_b)], axis=1)


def _retention_tables(c):
    log_g = jnp.log(1.0 - 2.0 ** (-5.0 - jnp.arange(B_HEADS, dtype=F32)))
    idx = jnp.arange(c, dtype=F32)
    rel = idx[:, None] - idx[None, :]
    d_in = jnp.where(rel >= 0, jnp.exp(log_g[:, None, None] * jnp.maximum(rel, 0.0)), 0.0)
    d_q = jnp.exp(log_g[:, None] * (idx + 1.0))
    d_k = jnp.exp(log_g[:, None] * (c - 1.0 - idx))
    d_c = jnp.exp(log_g * c)
    d_q = jnp.broadcast_to(d_q[:, :, None], (B_HEADS, c, LANES))
    d_k = jnp.broadcast_to(d_k[:, :, None], (B_HEADS, c, B_DK)).reshape(B_HEADS // 2, 2, c, B_DK)
    d_k = d_k.transpose(0, 2, 1, 3).reshape(B_HEADS // 2, c, 2 * B_DK)
    d_c = jnp.broadcast_to(d_c[:, None, None], (B_HEADS, B_DK, LANES)).reshape(B_HEADS // 2, 2 * B_DK, LANES)
    return d_in, d_q, d_k, d_c


def _rope_cols(r, c, p, n, half):
    outs = []
    for h in range(r.shape[1] // LANES):
        blk = r[:, h * LANES:(h + 1) * LANES]
        outs.append(blk * c + pltpu.roll(blk, half, 1) * p + pltpu.roll(blk, LANES - half, 1) * n)
    return jnp.concatenate(outs, axis=1)


def _sigmoid(x):
    return 1.0 / (1.0 + jnp.exp(-x))


def _inproj_kernel(x_ref, g_ref, w_ref, tab_ref, *refs, token_minor):
    xn_ref = refs[-1]
    if token_minor:
        qa_ref, kaf_ref, kab_ref, vaf_ref, vtb_ref, qkb_ref, vb_ref, gb_ref, qc_ref, gates_ref = refs[:-1]
    else:
        qa_ref, kaf_ref, vaf_ref, qkb_ref, vb_ref, gb_ref, qc_ref, gates_ref = refs[:-1]
    j = pl.program_id(1)

    @pl.when(j == 0)
    def _():
        x = x_ref[...]
        xn = x * lax.rsqrt(jnp.mean(x * x, axis=-1, keepdims=True) + EPS) * g_ref[...]
        xn_ref[...] = xn.astype(BF16)

    r = jnp.dot(xn_ref[...], w_ref[...], preferred_element_type=F32)

    def rope_a(v):
        return _rope_cols(v, tab_ref[:, 0:LANES], tab_ref[:, LANES:2 * LANES],
                          tab_ref[:, 2 * LANES:3 * LANES], A_ROT // 2)

    def rope_b(v):
        return _rope_cols(v, tab_ref[:, 3 * LANES:4 * LANES], tab_ref[:, 4 * LANES:5 * LANES],
                          tab_ref[:, 5 * LANES:6 * LANES], B_DK // 2)

    def finish_q(r):
        qa = rope_a(r) * (A_HD ** -0.5 * LOG2E)
        if token_minor:
            qa_ref[0] = qa.T.astype(BF16)
        else:
            qa_ref[...] = qa.astype(BF16)

    def finish_k(r):
        ka = rope_a(r)
        if token_minor:
            kaf_ref[0] = ka.T
            kab_ref[...] = ka.astype(BF16)
        else:
            kaf_ref[...] = ka

    def finish_v(r):
        vaf_ref[...] = r
        if token_minor:
            vtb_ref[0] = r.T.astype(BF16)

    def finish_qkb(r):
        rb = rope_b(r)
        half = SEG // 2
        qkb_ref[:, :half] = rb[:, :half].astype(BF16)
        qkb_ref[:, half:] = (rb[:, half:] * (B_DK ** -0.5)).astype(BF16)

    def finish_vb(r):
        vb_ref[...] = r.astype(BF16)

    def finish_gb(r):
        gb_ref[...] = (r * _sigmoid(r)).astype(BF16)

    def finish_qc(r):
        qc_ref[...] = (r * (C_HD ** -0.5)).astype(BF16)

    def finish_gates(r):
        gates_ref[...] = r.astype(BF16)

    finish = (finish_q, finish_k, finish_v, finish_qkb, finish_vb, finish_gb, finish_qc)

    for k in range(N_SEG):
        @pl.when(j == k)
        def _(k=k):
            finish[k](r)

    @pl.when(j >= N_SEG)
    def _():
        finish_gates(r)


def _inproj(x, g, w, tab, tm, seq_len=None):
    n, d = x.shape
    n_col = w.shape[1] // SEG
    n_gate = n_col - N_SEG
    tab_blocks = tab.shape[0] // tm
    token_minor = seq_len is not None
    rows = lambda dt: (jax.ShapeDtypeStruct((n, SEG), dt), pl.BlockSpec((tm, SEG), lambda i, j: (i, 0)))
    if token_minor:
        per_seq = seq_len // tm
        tmin = lambda dt: (jax.ShapeDtypeStruct((n // seq_len, SEG, seq_len), dt),
                           pl.BlockSpec((1, SEG, tm), lambda i, j: (i // per_seq, 0, i % per_seq)))
        outs = [tmin(BF16), tmin(F32), rows(BF16), rows(F32), tmin(BF16)]
    else:
        outs = [rows(BF16), rows(F32), rows(F32)]
    outs += [rows(BF16)] * 4
    outs.append((jax.ShapeDtypeStruct((n, n_gate * SEG), BF16),
                 pl.BlockSpec((tm, SEG), lambda i, j: (i, jnp.maximum(j - N_SEG, 0)))))
    out_shape = [o[0] for o in outs]
    out_specs = [o[1] for o in outs]
    return pl.pallas_call(
        functools.partial(_inproj_kernel, token_minor=token_minor),
        grid=(n // tm, n_col),
        in_specs=[
            pl.BlockSpec((tm, d), lambda i, j: (i, 0)),
            pl.BlockSpec((1, d), lambda i, j: (0, 0)),
            pl.BlockSpec((d, SEG), lambda i, j: (0, j)),
            pl.BlockSpec((tm, 6 * LANES), lambda i, j: (i % tab_blocks, 0)),
        ],
        out_specs=out_specs,
        out_shape=out_shape,
        scratch_shapes=[pltpu.VMEM((tm, d), BF16)],
        compiler_params=_cparams(("arbitrary", "arbitrary")),
        name="inproj",
    )(x, g, w, tab)


def _lambda(lq1_ref, lk1_ref, lq2_ref, lk2_ref, lam_init):
    a = jnp.sum(lq1_ref[...] * lk1_ref[...], axis=-1, keepdims=True)
    b = jnp.sum(lq2_ref[...] * lk2_ref[...], axis=-1, keepdims=True)
    return jnp.exp(a) - jnp.exp(b) + lam_init


def _subln(o, g, lam_init):
    y = o * lax.rsqrt(jnp.mean(o * o, axis=-1, keepdims=True) + EPS) * g
    return y * (1.0 - lam_init)


def _flash_kernel(lq1_ref, lk1_ref, lq2_ref, lk2_ref, gs_ref, qt_ref, k_ref, vt_ref, o_ref,
                  sa_ref, sb_ref, m_ref, l_ref, acc_ref, *, tq, lam_init):
    qi = pl.program_id(2)
    qt = qt_ref[0]
    feat = lax.broadcasted_iota(jnp.int32, qt.shape, 0)
    zero = jnp.zeros_like(qt)
    qs = jnp.concatenate([jnp.where(feat < A_HD, qt, zero), jnp.where(feat >= A_HD, qt, zero)], axis=1)

    s_refs = (sa_ref, sb_ref)

    def qk(kj, slot):
        start = pl.multiple_of(kj * tq, tq)
        s_refs[slot][...] = jnp.dot(k_ref[pl.ds(start, tq), :], qs, preferred_element_type=F32)

    def upd(kj, slot, diagonal):
        start = pl.multiple_of(kj * tq, tq)
        vb = vt_ref[0, :, pl.ds(start, tq)]
        s = s_refs[slot][...]
        if diagonal:
            key = lax.broadcasted_iota(jnp.int32, s.shape, 0)
            qry = lax.broadcasted_iota(jnp.int32, s.shape, 1) & (tq - 1)
            s = jnp.where(key <= qry, s, NEG_BIG)
        m = m_ref[...]
        m_new = jnp.maximum(m, jnp.max(s, axis=0, keepdims=True))
        corr = jnp.exp2(m - m_new)
        p = jnp.exp2(s - m_new)
        m_ref[...] = m_new
        l_ref[...] = l_ref[...] * corr + jnp.sum(p, axis=0, keepdims=True)
        acc_ref[...] = acc_ref[...] * corr + jnp.dot(vb, p.astype(BF16), preferred_element_type=F32)

    m_ref[...] = jnp.full(m_ref.shape, NEG_BIG, F32)
    l_ref[...] = jnp.zeros(l_ref.shape, F32)
    acc_ref[...] = jnp.zeros(acc_ref.shape, F32)
    qk(0, 0)

    def pair(pi, carry):
        b0 = 2 * pi
        qk(b0 + 1, 1)
        upd(b0, 0, False)
        qk(b0 + 2, 0)
        upd(b0 + 1, 1, False)
        return carry

    lax.fori_loop(0, qi // 2, pair, 0)

    @pl.when(qi % 2 == 1)
    def _():
        qk(qi, 1)
        upd(qi - 1, 0, False)
        upd(qi, 1, True)

    @pl.when(qi % 2 == 0)
    def _():
        upd(qi, 0, True)

    l = l_ref[...]
    acc = acc_ref[...]
    lam = _lambda(lq1_ref, lk1_ref, lq2_ref, lk2_ref, lam_init)
    o = acc[:, :tq] / l[:, :tq] - lam * (acc[:, tq:] / l[:, tq:])
    y = o * lax.rsqrt(jnp.mean(o * o, axis=0, keepdims=True) + EPS) * gs_ref[...] * (1.0 - lam_init)
    o_ref[...] = y.T.astype(BF16)


def _flash_diff_attn(lams, g_sub_col, qt, k, vt, b, t, tq, lam_init):
    assert tq & (tq - 1) == 0 and t % tq == 0
    nq = t // tq
    n = b * t
    vec = lambda shape: pl.BlockSpec(shape, lambda bi, h, qi: (0, 0))
    return pl.pallas_call(
        functools.partial(_flash_kernel, tq=tq, lam_init=lam_init),
        grid=(b, A_HEADS, nq),
        in_specs=[vec((1, A_HD))] * 4 + [
            vec((LANES, 1)),
            pl.BlockSpec((1, LANES, tq), lambda bi, h, qi: (bi, h, qi)),
            pl.BlockSpec((t, LANES), lambda bi, h, qi: (bi, h)),
            pl.BlockSpec((1, LANES, t), lambda bi, h, qi: (bi, h, 0)),
        ],
        out_specs=pl.BlockSpec((tq, LANES), lambda bi, h, qi: (bi * nq + qi, h)),
        out_shape=jax.ShapeDtypeStruct((n, A_HEADS * LANES), BF16),
        scratch_shapes=[pltpu.VMEM((tq, 2 * tq), F32), pltpu.VMEM((tq, 2 * tq), F32),
                        pltpu.VMEM((1, 2 * tq), F32), pltpu.VMEM((1, 2 * tq), F32),
                        pltpu.VMEM((LANES, 2 * tq), F32)],
        compiler_params=_cparams(("arbitrary", "arbitrary", "arbitrary")),
        name="flash_diff_attn",
    )(*lams, g_sub_col, qt, k, vt)


PAGED_ROWS = 2 * A_HEADS
PAGED_WIDTH = A_HEADS * LANES


def _paged_start(q_ref, kn_ref, vn_ref, qd_ref, m_ref, l_ref, acc_ref):
    rows, width = PAGED_ROWS, PAGED_WIDTH
    q = jnp.broadcast_to(q_ref[0].astype(F32), (rows, width))
    row = lax.broadcasted_iota(jnp.int32, (rows, width), 0)
    lane = lax.broadcasted_iota(jnp.int32, (rows, width), 1)
    qd = jnp.where(lane // A_HD == row, q, 0.0)
    qd_ref[...] = qd.astype(BF16)
    s_self = jnp.sum(qd * kn_ref[0], axis=-1, keepdims=True)
    m_ref[...] = jnp.broadcast_to(s_self, (rows, LANES))
    l_ref[...] = jnp.ones((rows, LANES), F32)
    acc_ref[...] = jnp.broadcast_to(vn_ref[0], (rows, width))


def _paged_scores(k_refs, qd_ref):
    qd = qd_ref[...]
    return jnp.concatenate([jnp.dot(qd, k_ref[0].astype(BF16), preferred_element_type=F32)
                            for k_ref in k_refs], axis=1)


def _paged_softmax(s, m_ref, l_ref):
    rows = PAGED_ROWS
    m = m_ref[:, 0:1]
    m_new = jnp.maximum(m, jnp.max(s, axis=-1, keepdims=True))
    corr = jnp.exp2(m - m_new)
    p = jnp.exp2(s - m_new)
    l = l_ref[:, 0:1] * corr + jnp.sum(p, axis=-1, keepdims=True)
    m_ref[...] = jnp.broadcast_to(m_new, (rows, LANES))
    l_ref[...] = jnp.broadcast_to(l, (rows, LANES))
    return p.astype(BF16), corr


def _paged_values(v_refs, pb, heads):
    page = v_refs[0].shape[1] // A_HEADS
    outs = []
    for h in heads:
        vh = jnp.concatenate([v_ref[0, pl.ds(h, page, stride=A_HEADS), :].astype(BF16)
                              for v_ref in v_refs], axis=0)
        outs.append(jnp.dot(pb, vh, preferred_element_type=F32))
    return outs


def _paged_accumulate(acc_ref, corr, outs, heads):
    for h, o in zip(heads, outs):
        cols = slice(h * LANES, (h + 1) * LANES)
        acc_ref[:, cols] = acc_ref[:, cols] * corr + o


def _paged_pages(k_refs, v_refs, qd_ref, m_ref, l_ref, acc_ref):
    pb, corr = _paged_softmax(_paged_scores(k_refs, qd_ref), m_ref, l_ref)
    heads = range(A_HEADS)
    _paged_accumulate(acc_ref, corr, _paged_values(v_refs, pb, heads), heads)


def _paged_finish(lam_refs, gs_ref, l_ref, acc_ref, o_ref, lam_init):
    lam = _lambda(*lam_refs, lam_init)
    g = gs_ref[...]
    outs = []
    for h in range(A_HEADS):
        cols = slice(h * LANES, (h + 1) * LANES)
        a0 = acc_ref[2 * h:2 * h + 1, cols] / l_ref[2 * h:2 * h + 1, 0:1]
        a1 = acc_ref[2 * h + 1:2 * h + 2, cols] / l_ref[2 * h + 1:2 * h + 2, 0:1]
        outs.append(_subln(a0 - lam * a1, g, lam_init))
    o_ref[0] = jnp.concatenate(outs, axis=1).astype(BF16)


def _paged_kernel(pt_ref, lq1_ref, lk1_ref, lq2_ref, lk2_ref, gs_ref, q_ref, kn_ref, vn_ref, *rest,
                  pages, lam_init):
    k_refs = rest[:pages]
    v_refs = rest[pages:2 * pages]
    o_ref = rest[2 * pages]
    state = rest[2 * pages + 1:]
    j = pl.program_id(1)

    @pl.when(j == 0)
    def _():
        _paged_start(q_ref, kn_ref, vn_ref, *state)

    _paged_pages(k_refs, v_refs, *state)

    @pl.when(j == pl.num_programs(1) - 1)
    def _():
        _paged_finish((lq1_ref, lk1_ref, lq2_ref, lk2_ref), gs_ref, state[2], state[3], o_ref, lam_init)


def _paged_specs(pages, width, page, seq_of, group_of):
    vec = lambda shape: pl.BlockSpec(shape, lambda a, b, pt: (0, 0))
    row3 = pl.BlockSpec((1, 1, width), lambda a, b, pt: (seq_of(a, b), 0, 0))

    def page_spec(r, shape):
        return pl.BlockSpec((1,) + shape,
                            lambda a, b, pt: (pt[seq_of(a, b), group_of(a, b) * pages + r], 0, 0))

    in_specs = ([vec((1, A_HD))] * 4 + [vec((1, LANES)), row3, row3, row3]
                + [page_spec(r, (width, page)) for r in range(pages)]
                + [page_spec(r, (page * A_HEADS, LANES)) for r in range(pages)])
    scratch = [pltpu.VMEM((PAGED_ROWS, width), BF16), pltpu.VMEM((PAGED_ROWS, LANES), F32),
               pltpu.VMEM((PAGED_ROWS, LANES), F32), pltpu.VMEM((PAGED_ROWS, width), F32)]
    return in_specs, row3, scratch


def _paged_diff_attn(page_table, lams, g_sub, q, k_new, v_new, cache_kt, cache_v, pages, lam_init):
    db, n_pages = page_table.shape
    width, page = cache_kt.shape[1:]
    assert n_pages % pages == 0 and cache_v.shape[1:] == (page * A_HEADS, LANES) and width == PAGED_WIDTH
    in_specs, out_spec, scratch = _paged_specs(pages, width, page, lambda a, b: a, lambda a, b: b)
    grid_spec = pltpu.PrefetchScalarGridSpec(
        num_scalar_prefetch=1, grid=(db, n_pages // pages),
        in_specs=in_specs, out_specs=out_spec, scratch_shapes=scratch)
    return pl.pallas_call(
        functools.partial(_paged_kernel, pages=pages, lam_init=lam_init),
        grid_spec=grid_spec,
        out_shape=jax.ShapeDtypeStruct((db, 1, width), BF16),
        compiler_params=_cparams(("arbitrary", "arbitrary")),
        name="paged_diff_attn",
    )(page_table, *lams, g_sub, q, k_new, v_new, *([cache_kt] * pages), *([cache_v] * pages))


def _head_mask(x, e):
    lane = lax.broadcasted_iota(jnp.int32, x.shape, x.ndim - 1)
    keep = (lane >= B_DK) if e else (lane < B_DK)
    return jnp.where(keep, x, jnp.zeros_like(x))


def _rms_plain(o):
    return o * lax.rsqrt(jnp.mean(o * o, axis=-1, keepdims=True) + EPS)


def _retention_kernel(qk_ref, v_ref, gs_ref, din_ref, dq_ref, dk_ref, dc_ref, o_ref, s_out_ref, s_ref, *, c):
    ci = pl.program_id(1)

    @pl.when(ci == 0)
    def _():
        s_ref[...] = jnp.zeros_like(s_ref)

    half = B_HEADS * B_DK
    n_sub = qk_ref.shape[0] // c
    for p in range(B_HEADS // 2):
        state = s_ref[p]
        for sub in range(n_sub):
            tok = slice(sub * c, (sub + 1) * c)
            q2 = qk_ref[tok, p * LANES:(p + 1) * LANES]
            k2 = qk_ref[tok, half + p * LANES:half + (p + 1) * LANES]
            s_bf = state.astype(BF16)
            kd = (k2.astype(F32) * dk_ref[p]).astype(BF16)
            new_rows = []
            for e in range(2):
                h = 2 * p + e
                qz = _head_mask(q2, e)
                vh = v_ref[tok, h * LANES:(h + 1) * LANES]
                attn = lax.dot_general(qz, k2, NT_DIMS, preferred_element_type=F32) * din_ref[h]
                o = (jnp.dot(attn.astype(BF16), vh, preferred_element_type=F32)
                     + jnp.dot(qz, s_bf, preferred_element_type=F32) * dq_ref[h])
                u = lax.dot_general(kd, vh, TN_DIMS, preferred_element_type=F32)
                rows = slice(e * B_DK, (e + 1) * B_DK)
                new_rows.append(state[rows] * dc_ref[p][rows] + u[rows])
                gate = gs_ref[tok, h * LANES:(h + 1) * LANES].astype(F32)
                o_ref[tok, h * LANES:(h + 1) * LANES] = (_rms_plain(o) * gate).astype(BF16)
            state = jnp.concatenate(new_rows, axis=0)
        s_ref[p] = state

    @pl.when(ci == pl.num_programs(1) - 1)
    def _():
        s_out_ref[0] = s_ref[...].reshape(B_HEADS, B_DK, B_DV)


def _retention_prompt(qk, v, gs, b, t):
    c = math.gcd(t, RET_CHUNK)
    blk = math.gcd(t, RET_CHUNKS_PER_STEP * c)
    nc = t // blk
    n = b * t
    d_in, d_q, d_k, d_c = _retention_tables(c)
    rows = lambda bi, ci: (bi * nc + ci, 0)
    const3 = lambda bi, ci: (0, 0, 0)
    width = B_HEADS * B_DV
    return pl.pallas_call(
        functools.partial(_retention_kernel, c=c),
        grid=(b, nc),
        in_specs=[
            pl.BlockSpec((blk, 2 * B_HEADS * B_DK), rows),
            pl.BlockSpec((blk, width), rows),
            pl.BlockSpec((blk, width), rows),
            pl.BlockSpec(d_in.shape, const3),
            pl.BlockSpec(d_q.shape, const3),
            pl.BlockSpec(d_k.shape, const3),
            pl.BlockSpec(d_c.shape, const3),
        ],
        out_specs=[
            pl.BlockSpec((blk, width), rows),
            pl.BlockSpec((1, B_HEADS, B_DK, B_DV), lambda bi, ci: (bi, 0, 0, 0)),
        ],
        out_shape=[jax.ShapeDtypeStruct((n, width), BF16),
                   jax.ShapeDtypeStruct((b, B_HEADS, B_DK, B_DV), F32)],
        scratch_shapes=[pltpu.VMEM((B_HEADS // 2, 2 * B_DK, B_DV), F32)],
        compiler_params=_cparams(("arbitrary", "arbitrary")),
        name="retention_prompt",
    )(qk, v, gs, d_in, d_q, d_k, d_c)


def _retention_step_kernel(qk_ref, v_ref, gs_ref, s0_ref, dq_ref, dc_ref, o_ref, s_out_ref):
    half = B_HEADS * B_DK
    pad = 16
    row0 = lax.broadcasted_iota(jnp.int32, (pad, LANES), 0) == 0
    qk = qk_ref[0].astype(F32)
    v = v_ref[0].astype(F32)

    def pad_rows(x):
        return jnp.where(row0, jnp.broadcast_to(x, (pad, LANES)), 0.0).astype(BF16)

    for p in range(B_HEADS // 2):
        q2 = qk[:, p * LANES:(p + 1) * LANES]
        k2 = qk[:, half + p * LANES:half + (p + 1) * LANES]
        s_old = s0_ref[0, 2 * p:2 * p + 2].reshape(2 * B_DK, B_DV)
        s_bf = s_old.astype(BF16)
        k8 = pad_rows(k2)
        new_rows = []
        for e in range(2):
            h = 2 * p + e
            qz = _head_mask(q2, e)
            vh = v[:, h * LANES:(h + 1) * LANES]
            qk_dot = jnp.sum(qz * k2, axis=-1, keepdims=True)
            qs = jnp.dot(pad_rows(qz), s_bf, preferred_element_type=F32)[0:1]
            o = qk_dot * vh + qs * dq_ref[h][0:1]
            u = lax.dot_general(k8, pad_rows(vh), TN_DIMS, preferred_element_type=F32)
            rows = slice(e * B_DK, (e + 1) * B_DK)
            new_rows.append(s_old[rows] * dc_ref[p][rows] + u[rows])
            gate = gs_ref[0][:, h * LANES:(h + 1) * LANES].astype(F32)
            o_ref[0, :, h * LANES:(h + 1) * LANES] = (_rms_plain(o) * gate).astype(BF16)
        s_out_ref[0, 2 * p:2 * p + 2] = jnp.concatenate(new_rows, axis=0).reshape(2, B_DK, B_DV)


def _retention_sample(qk, v, gs, s0):
    db = qk.shape[0]
    _, d_q, _, d_c = _retention_tables(1)
    width = B_HEADS * B_DV
    row3 = lambda w: pl.BlockSpec((1, 1, w), lambda bi: (bi, 0, 0))
    state = pl.BlockSpec((1, B_HEADS, B_DK, B_DV), lambda bi: (bi, 0, 0, 0))
    const3 = lambda bi: (0, 0, 0)
    return pl.pallas_call(
        _retention_step_kernel,
        grid=(db,),
        in_specs=[row3(2 * B_HEADS * B_DK), row3(width), row3(width), state,
                  pl.BlockSpec(d_q.shape, const3), pl.BlockSpec(d_c.shape, const3)],
        out_specs=[row3(width), state],
        out_shape=[jax.ShapeDtypeStruct((db, 1, width), BF16),
                   jax.ShapeDtypeStruct(s0.shape, F32)],
        compiler_params=_cparams(("arbitrary",)),
        name="retention_sample",
    )(qk, v, gs, s0, d_q, d_c)


def _rms_matmul_kernel(x_ref, g_ref, w_ref, of_ref, ob_ref, xn_ref):
    @pl.when(pl.program_id(1) == 0)
    def _():
        x = x_ref[...]
        xn = x * lax.rsqrt(jnp.mean(x * x, axis=-1, keepdims=True) + EPS) * g_ref[...]
        xn_ref[...] = xn.astype(BF16)

    r = jnp.dot(xn_ref[...], w_ref[...], preferred_element_type=F32)
    of_ref[...] = r
    ob_ref[...] = r.astype(BF16)


def _rms_matmul(x, g, w, tm, tn):
    n, d = x.shape
    width = w.shape[1]
    return pl.pallas_call(
        _rms_matmul_kernel,
        grid=(n // tm, width // tn),
        in_specs=[pl.BlockSpec((tm, d), lambda i, j: (i, 0)),
                  pl.BlockSpec((1, d), lambda i, j: (0, 0)),
                  pl.BlockSpec((d, tn), lambda i, j: (0, j))],
        out_specs=[pl.BlockSpec((tm, tn), lambda i, j: (i, j))] * 2,
        out_shape=[jax.ShapeDtypeStruct((n, width), F32), jax.ShapeDtypeStruct((n, width), BF16)],
        scratch_shapes=[pltpu.VMEM((tm, d), BF16)],
        compiler_params=_cparams(("arbitrary", "arbitrary")),
        name="mem_kv",
    )(x, g, w)


def _softmax_rows(s):
    m = jnp.max(s, axis=-1, keepdims=True)
    p = jnp.exp(s - m)
    return p / jnp.sum(p, axis=-1, keepdims=True)


def _mem_attn_kernel(q_ref, mk_ref, mv_ref, o_ref):
    for h in range(C_HEADS):
        cols = slice(h * C_HD, (h + 1) * C_HD)
        s = lax.dot_general(q_ref[:, cols], mk_ref[:, cols], NT_DIMS, preferred_element_type=F32)
        p = _softmax_rows(s)
        o_ref[:, cols] = jnp.dot(p.astype(BF16), mv_ref[:, cols], preferred_element_type=F32).astype(BF16)


def _mem_attn_prompt(q, mk, mv, t, tm):
    n, width = q.shape
    mem_len = mk.shape[0] // (n // t)
    per_seq = t // tm
    return pl.pallas_call(
        _mem_attn_kernel,
        grid=(n // tm,),
        in_specs=[pl.BlockSpec((tm, width), lambda i: (i, 0)),
                  pl.BlockSpec((mem_len, width), lambda i: (i // per_seq, 0)),
                  pl.BlockSpec((mem_len, width), lambda i: (i // per_seq, 0))],
        out_specs=pl.BlockSpec((tm, width), lambda i: (i, 0)),
        out_shape=jax.ShapeDtypeStruct((n, width), BF16),
        compiler_params=_cparams(("arbitrary",)),
        name="mem_attn_prompt",
    )(q, mk, mv)


def _mem_attn_step_kernel(q_ref, mk_ref, mv_ref, o_ref):
    pad = 16
    width = C_HEADS * C_HD
    q = jnp.broadcast_to(q_ref[0].astype(F32), (pad, width))
    row = lax.broadcasted_iota(jnp.int32, (pad, width), 0)
    lane = lax.broadcasted_iota(jnp.int32, (pad, width), 1)
    qd = jnp.where(lane // C_HD == row, q, 0.0).astype(BF16)
    s = lax.dot_general(qd, mk_ref[0].astype(BF16), NT_DIMS, preferred_element_type=F32)
    p = _softmax_rows(s)
    o = jnp.dot(p.astype(BF16), mv_ref[0].astype(BF16), preferred_element_type=F32)
    o_ref[0] = jnp.concatenate([o[h:h + 1, h * C_HD:(h + 1) * C_HD] for h in range(C_HEADS)],
                               axis=1).astype(BF16)


def _mem_attn_sample(q, mk, mv):
    db, mem_len, width = mk.shape
    row3 = pl.BlockSpec((1, 1, width), lambda bi: (bi, 0, 0))
    mem = pl.BlockSpec((1, mem_len, width), lambda bi: (bi, 0, 0))
    return pl.pallas_call(
        _mem_attn_step_kernel,
        grid=(db,),
        in_specs=[row3, mem, mem],
        out_specs=row3,
        out_shape=jax.ShapeDtypeStruct((db, 1, width), BF16),
        compiler_params=_cparams(("arbitrary",)),
        name="mem_attn_sample",
    )(q, mk, mv)


def _merge_kernel(x_ref, oa_ref, ob_ref, oc_ref, gates_ref, pa_ref, pb_ref, pc_ref, wo_ref, h_ref):
    d = x_ref.shape[1]
    gate = lambda k: _sigmoid(gates_ref[:, k * d:(k + 1) * d].astype(F32))
    merged = (gate(0) * jnp.dot(oa_ref[...], pa_ref[...], preferred_element_type=F32)
              + gate(1) * jnp.dot(ob_ref[...], pb_ref[...], preferred_element_type=F32)
              + gate(2) * jnp.dot(oc_ref[...], pc_ref[...], preferred_element_type=F32))
    h_ref[...] = x_ref[...] + jnp.dot(merged.astype(BF16), wo_ref[...], preferred_element_type=F32)


def _merge(x, oa, ob, oc, gates, p_a, p_b, p_c, w_o, tm):
    n, d = x.shape
    rows = lambda w: pl.BlockSpec((tm, w), lambda i: (i, 0))
    whole = lambda a: _resident(a.shape, lambda i: (0, 0))
    return pl.pallas_call(
        _merge_kernel,
        grid=(n // tm,),
        in_specs=[rows(d), rows(oa.shape[1]), rows(ob.shape[1]), rows(oc.shape[1]), rows(gates.shape[1]),
                  whole(p_a), whole(p_b), whole(p_c), whole(w_o)],
        out_specs=rows(d),
        out_shape=jax.ShapeDtypeStruct((n, d), F32),
        compiler_params=_cparams(("arbitrary",)),
        name="merge",
    )(x, oa, ob, oc, gates, p_a, p_b, p_c, w_o)


def _mlp_kernel(h_ref, g_ref, wu_ref, wd_ref, gf_ref, y_ref, hn_ref, acc_ref):
    f = pl.program_id(1)

    @pl.when(f == 0)
    def _():
        h = h_ref[...]
        hn = h * lax.rsqrt(jnp.mean(h * h, axis=-1, keepdims=True) + EPS) * g_ref[...]
        hn_ref[...] = hn.astype(BF16)
        acc_ref[...] = jnp.zeros_like(acc_ref)

    u = jnp.dot(hn_ref[...], wu_ref[...], preferred_element_type=F32)
    a = jnp.square(jnp.maximum(u, 0.0)).astype(BF16)
    acc_ref[...] += jnp.dot(a, wd_ref[...], preferred_element_type=F32)

    @pl.when(f == pl.num_programs(1) - 1)
    def _():
        h2 = h_ref[...] + acc_ref[...]
        y_ref[...] = h2 * lax.rsqrt(jnp.mean(h2 * h2, axis=-1, keepdims=True) + EPS) * gf_ref[...]


def _mlp_paged_kernel(pt_ref, h_ref, g_ref, wu_ref, wd_ref, gf_ref, lq1_ref, lk1_ref, lq2_ref, lk2_ref,
                      gs_ref, q_ref, kn_ref, vn_ref, *rest, pages, lam_init):
    k_refs = rest[:pages]
    v_refs = rest[pages:2 * pages]
    y_ref, o_ref, hn_ref, acc_ref = rest[2 * pages:2 * pages + 4]
    state = rest[2 * pages + 4:]
    f = pl.program_id(1)
    last = pl.num_programs(1) - 1

    @pl.when(f == 0)
    def _():
        h = h_ref[...]
        hn = h * lax.rsqrt(jnp.mean(h * h, axis=-1, keepdims=True) + EPS) * g_ref[...]
        hn_ref[...] = hn.astype(BF16)
        acc_ref[...] = jnp.zeros_like(acc_ref)
        _paged_start(q_ref, kn_ref, vn_ref, *state)

    qd_ref, pm_ref, pl_ref, pacc_ref = state
    s = _paged_scores(k_refs, qd_ref)
    u = jnp.dot(hn_ref[...], wu_ref[...], preferred_element_type=F32)
    pb, corr = _paged_softmax(s, pm_ref, pl_ref)
    a = jnp.square(jnp.maximum(u, 0.0)).astype(BF16)
    n_chunk = A_HEADS // 2
    cw = acc_ref.shape[1] // n_chunk
    for c in range(n_chunk):
        heads = (2 * c, 2 * c + 1)
        outs = _paged_values(v_refs, pb, heads)
        cols = slice(c * cw, (c + 1) * cw)
        acc_ref[:, cols] += jnp.dot(a, wd_ref[:, cols], preferred_element_type=F32)
        _paged_accumulate(pacc_ref, corr, outs, heads)

    @pl.when(f == last)
    def _():
        h2 = h_ref[...] + acc_ref[...]
        y_ref[...] = h2 * lax.rsqrt(jnp.mean(h2 * h2, axis=-1, keepdims=True) + EPS) * gf_ref[...]
        _paged_finish((lq1_ref, lk1_ref, lq2_ref, lk2_ref), gs_ref, state[2], state[3], o_ref, lam_init)


def _mlp_with_paged(h, g_mlp, w_up, w_down, g_final, tm, tf,
                    page_table, lams, g_sub, q, k_new, v_new, cache_kt, cache_v, lam_init):
    n, d = h.shape
    d_ff = w_up.shape[1]
    db, n_pages = page_table.shape
    n_f = d_ff // tf
    width, page = cache_kt.shape[1:]
    assert n // tm == db and n_pages % n_f == 0
    pages = n_pages // n_f
    paged_in, paged_out, paged_scratch = _paged_specs(pages, width, page, lambda i, f: i, lambda i, f: f)
    grid_spec = pltpu.PrefetchScalarGridSpec(
        num_scalar_prefetch=1,
        grid=(n // tm, n_f),
        in_specs=[pl.BlockSpec((tm, d), lambda i, f, pt: (i, 0)),
                  pl.BlockSpec((1, d), lambda i, f, pt: (0, 0)),
                  pl.BlockSpec((d, tf), lambda i, f, pt: (0, f)),
                  pl.BlockSpec((tf, d), lambda i, f, pt: (f, 0)),
                  pl.BlockSpec((1, d), lambda i, f, pt: (0, 0))] + paged_in,
        out_specs=[pl.BlockSpec((tm, d), lambda i, f, pt: (i, 0)), paged_out],
        scratch_shapes=[pltpu.VMEM((tm, d), BF16), pltpu.VMEM((tm, d), F32)] + paged_scratch,
    )
    return pl.pallas_call(
        functools.partial(_mlp_paged_kernel, pages=pages, lam_init=lam_init),
        grid_spec=grid_spec,
        out_shape=[jax.ShapeDtypeStruct((n, d), F32), jax.ShapeDtypeStruct((db, 1, width), BF16)],
        compiler_params=_cparams(("arbitrary", "arbitrary")),
        name="mlp_paged",
    )(page_table, h, g_mlp, w_up, w_down, g_final, *lams, g_sub, q, k_new, v_new,
      *([cache_kt] * pages), *([cache_v] * pages))


def _mlp(h, g_mlp, w_up, w_down, g_final, tm, tf):
    n, d = h.shape
    d_ff = w_up.shape[1]
    return pl.pallas_call(
        _mlp_kernel,
        grid=(n // tm, d_ff // tf),
        in_specs=[pl.BlockSpec((tm, d), lambda i, f: (i, 0)),
                  pl.BlockSpec((1, d), lambda i, f: (0, 0)),
                  pl.BlockSpec((d, tf), lambda i, f: (0, f)),
                  pl.BlockSpec((tf, d), lambda i, f: (f, 0)),
                  pl.BlockSpec((1, d), lambda i, f: (0, 0))],
        out_specs=pl.BlockSpec((tm, d), lambda i, f: (i, 0)),
        out_shape=jax.ShapeDtypeStruct((n, d), F32),
        scratch_shapes=[pltpu.VMEM((tm, d), BF16), pltpu.VMEM((tm, d), F32)],
        compiler_params=_cparams(("arbitrary", "arbitrary")),
        name="mlp",
    )(h, g_mlp, w_up, w_down, g_final)


def _row_tile(n, target):
    return min(n, target)


def kernel(x_prompt, x_sample, cache_diff_k, cache_diff_v, state_ret, cache_mem_k, cache_mem_v, page_table, mem_prompt, g_mix, w_in, lam_q1, lam_k1, lam_q2, lam_k2, g_subln, g_mem, w_mem_k, w_mem_v, p_a, p_b, p_c, w_o, g_mlp, w_up, w_down, g_final):
    depth = g_mix.shape[0]
    assert depth == 1
    layer = 0
    lam_init = 0.8 - 0.6 * math.exp(-0.3 * layer)
    b, t, d = x_prompt.shape
    db, dt, _ = x_sample.shape
    assert dt == 1
    n_pool, page = cache_diff_k.shape[1:3]
    past_len = page_table.shape[1] * page
    mem_len = mem_prompt.shape[1]
    n_p, n_s = b * t, db * dt

    row = lambda a: a[layer].reshape(1, -1)
    lams = (row(lam_q1), row(lam_k1), row(lam_q2), row(lam_k2))
    g_sub = row(g_subln)
    w_in_b = w_in[layer].astype(BF16)
    w_mem_b = jnp.concatenate([w_mem_k[layer], w_mem_v[layer]], axis=1).astype(BF16)
    pa_b, pb_b, pc_b, wo_b = (a[layer].astype(BF16) for a in (p_a, p_b, p_c, w_o))
    wu_b, wd_b = w_up[layer].astype(BF16), w_down[layer].astype(BF16)
    g_final2 = g_final.reshape(1, -1)

    tm_p = _row_tile(t, 512)
    xp = x_prompt.reshape(n_p, d)
    tab_p = _rope_lane_tables(jnp.arange(t))
    qt, ktf, kab, vaf, vtb, qkb, vb, gsb, qc, gates = _inproj(xp, row(g_mix), w_in_b, tab_p, tm_p, seq_len=t)
    oa = _flash_diff_attn(lams, g_sub.reshape(-1, 1), qt, kab, vtb, b, t, _row_tile(t, 512), lam_init)
    ob, state_p = _retention_prompt(qkb, vb, gsb, b, t)
    mem_f, mem_b = _rms_matmul(mem_prompt.reshape(b * mem_len, d), row(g_mem), w_mem_b,
                               _row_tile(b * mem_len, 256), 1024)
    c_w = C_HEADS * C_HD
    oc = _mem_attn_prompt(qc, mem_b[:, :c_w], mem_b[:, c_w:], t, tm_p)
    hp = _merge(xp, oa, ob, oc, gates, pa_b, pb_b, pc_b, wo_b, _row_tile(t, 256))

    xs = x_sample.reshape(n_s, d)
    tab_s = jnp.tile(_rope_lane_tables(past_len + jnp.arange(dt)), (db, 1))
    qa_s, kaf_s, vaf_s, qkb_s, vb_s, gsb_s, qc_s, gates_s = _inproj(xs, row(g_mix), w_in_b, tab_s, n_s)
    width_a = A_HEADS * LANES
    r3 = lambda a: a.reshape(db, 1, a.shape[-1])
    ckt = cache_diff_k[layer].transpose(0, 2, 3, 4, 1).reshape(n_pool, width_a, page)
    cv = cache_diff_v[layer].reshape(n_pool, page * A_HEADS, LANES)
    paged_args = (page_table, lams, g_sub, r3(qa_s), r3(kaf_s), r3(vaf_s), ckt, cv)
    tf_fused = 512
    d_ff = wu_b.shape[1]
    if n_p // tm_p == db and page_table.shape[1] % (d_ff // tf_fused) == 0:
        y_p, oa_s = _mlp_with_paged(hp, row(g_mlp), wu_b, wd_b, g_final2, tm_p, tf_fused, *paged_args, lam_init)
    else:
        y_p = _mlp(hp, row(g_mlp), wu_b, wd_b, g_final2, tm_p, 1024)
        oa_s = _paged_diff_attn(*paged_args, 8, lam_init)
    ob_s, state_s = _retention_sample(r3(qkb_s), r3(vb_s), r3(gsb_s), state_ret[layer])
    oc_s = _mem_attn_sample(r3(qc_s), cache_mem_k[layer].reshape(db, mem_len, c_w),
                            cache_mem_v[layer].reshape(db, mem_len, c_w))
    hs = _merge(xs, oa_s.reshape(n_s, -1), ob_s.reshape(n_s, -1), oc_s.reshape(n_s, -1), gates_s,
                pa_b, pb_b, pc_b, wo_b, n_s)
    y_s = _mlp(hs, row(g_mlp), wu_b, wd_b, g_final2, n_s, 1024)

    return (
        y_p.reshape(b, t, d),
        y_s.reshape(db, dt, d),
        ktf.reshape(1, b, A_HEADS, 2, A_HD, t).transpose(0, 1, 5, 2, 3, 4),
        vaf.reshape(1, b, t, A_HEADS, 2 * A_HD),
        state_p[None],
        mem_f[:, :c_w].reshape(1, b, mem_len, C_HEADS, C_HD),
        mem_f[:, c_w:].reshape(1, b, mem_len, C_HEADS, C_HD),
        kaf_s.reshape(1, db, dt, A_HEADS, 2, A_HD),
        vaf_s.reshape(1, db, dt, A_HEADS, 2 * A_HD),
        state_s[None],
    )
```

```python
import functools
import math
from typing import NamedTuple

import jax
import jax.numpy as jnp
from jax import lax
from jax.experimental import pallas as pl
from jax.experimental.pallas import tpu as pltpu

F32 = jnp.float32
BF16 = jnp.bfloat16
EPS = 1e-6
NEG_BIG = -1e30
LOG2E = math.log2(math.e)

LANES = 128
BF16_SUBLANES = 16
VMEM_LIMIT = 56 * 1024 * 1024

A_HEADS = 8
A_HD = 64
A_ROT = A_HD // 4
ROPE_THETA = 500000.0
B_HEADS = 8
B_DK = 64
B_DV = 128
RET_THETA = 10000.0
RET_CHUNK = 128
RET_CHUNKS_PER_STEP = 4
C_HEADS = 4
C_HD = 256
SEG = 1024
N_SEG = 7

NT_DIMS = (((1,), (1,)), ((), ()))
TN_DIMS = (((0,), (0,)), ((), ()))


def _cparams(sem):
    return pltpu.CompilerParams(dimension_semantics=sem, vmem_limit_bytes=VMEM_LIMIT)


def _resident(shape, index_map):
    return pl.BlockSpec(shape, index_map, pipeline_mode=pl.Buffered(1))


def _rope_angles(pos, n_rot, theta):
    inv = 1.0 / (theta ** (jnp.arange(0, n_rot, 2, dtype=F32) / n_rot))
    ang = pos.astype(F32)[:, None] * inv[None, :]
    return jnp.cos(ang), jnp.sin(ang)


def _rope_lane_tables(pos):
    t = pos.shape[0]
    ca, sa = _rope_angles(pos, A_ROT, ROPE_THETA)
    ha = A_ROT // 2
    c_a = jnp.concatenate([ca, ca, jnp.ones((t, A_HD - A_ROT), F32)], axis=1)
    p_a = jnp.concatenate([jnp.zeros((t, ha), F32), sa, jnp.zeros((t, A_HD - A_ROT), F32)], axis=1)
    n_a = jnp.concatenate([-sa, jnp.zeros((t, A_HD - ha), F32)], axis=1)
    cb, sb = _rope_angles(pos, B_DK, RET_THETA)
    hb = B_DK // 2
    c_b = jnp.concatenate([cb, cb], axis=1)
    p_b = jnp.concatenate([jnp.zeros((t, hb), F32), sb], axis=1)
    n_b = jnp.concatenate([-sb, jnp.zeros((t, hb), F32)], axis=1)
    return jnp.concatenate([jnp.tile(a, (1, 2)) for a in (c_a, p_a, n_a, c_b, p_b, n_b)], axis=1)


def _retention_tables(c):
    log_g = jnp.log(1.0 - 2.0 ** (-5.0 - jnp.arange(B_HEADS, dtype=F32)))
    idx = jnp.arange(c, dtype=F32)
    rel = idx[:, None] - idx[None, :]
    d_in = jnp.where(rel >= 0, jnp.exp(log_g[:, None, None] * jnp.maximum(rel, 0.0)), 0.0)
    d_q = jnp.exp(log_g[:, None] * (idx + 1.0))
    d_k = jnp.exp(log_g[:, None] * (c - 1.0 - idx))
    d_c = jnp.exp(log_g * c)
    d_q = jnp.broadcast_to(d_q[:, :, None], (B_HEADS, c, LANES))
    d_k = jnp.broadcast_to(d_k[:, :, None], (B_HEADS, c, B_DK)).reshape(B_HEADS // 2, 2, c, B_DK)
    d_k = d_k.transpose(0, 2, 1, 3).reshape(B_HEADS // 2, c, 2 * B_DK)
    d_c = jnp.broadcast_to(d_c[:, None, None], (B_HEADS, B_DK, LANES)).reshape(B_HEADS // 2, 2 * B_DK, LANES)
    return d_in, d_q, d_k, d_c


def _rope_cols(r, c, p, n, half):
    outs = []
    for h in range(r.shape[1] // LANES):
        blk = r[:, h * LANES:(h + 1) * LANES]
        outs.append(blk * c + pltpu.roll(blk, half, 1) * p + pltpu.roll(blk, LANES - half, 1) * n)
    return jnp.concatenate(outs, axis=1)


def _sigmoid(x):
    return 1.0 / (1.0 + jnp.exp(-x))


def _inproj_kernel(x_ref, g_ref, w_ref, tab_ref, *refs, token_minor):
    xn_ref = refs[-1]
    if token_minor:
        qa_ref, kaf_ref, kab_ref, vaf_ref, vtb_ref, qkb_ref, vb_ref, gb_ref, qc_ref, gates_ref = refs[:-1]
    else:
        qa_ref, kaf_ref, vaf_ref, qkb_ref, vb_ref, gb_ref, qc_ref, gates_ref = refs[:-1]
    j = pl.program_id(1)

    @pl.when(j == 0)
    def _():
        x = x_ref[...]
        xn = x * lax.rsqrt(jnp.mean(x * x, axis=-1, keepdims=True) + EPS) * g_ref[...]
        xn_ref[...] = xn.astype(BF16)

    r = jnp.dot(xn_ref[...], w_ref[...], preferred_element_type=F32)

    def rope_a(v):
        return _rope_cols(v, tab_ref[:, 0:LANES], tab_ref[:, LANES:2 * LANES],
                          tab_ref[:, 2 * LANES:3 * LANES], A_ROT // 2)

    def rope_b(v):
        return _rope_cols(v, tab_ref[:, 3 * LANES:4 * LANES], tab_ref[:, 4 * LANES:5 * LANES],
                          tab_ref[:, 5 * LANES:6 * LANES], B_DK // 2)

    def finish_q(r):
        qa = rope_a(r) * (A_HD ** -0.5 * LOG2E)
        if token_minor:
            qa_ref[0] = qa.T.astype(BF16)
        else:
            qa_ref[...] = qa.astype(BF16)

    def finish_k(r):
        ka = rope_a(r)
        if token_minor:
            kaf_ref[0] = ka.T
            kab_ref[...] = ka.astype(BF16)
        else:
            kaf_ref[...] = ka

    def finish_v(r):
        vaf_ref[...] = r
        if token_minor:
            vtb_ref[0] = r.T.astype(BF16)

    def finish_qkb(r):
        rb = rope_b(r)
        half = SEG // 2
        qkb_ref[:, :half] = rb[:, :half].astype(BF16)
        qkb_ref[:, half:] = (rb[:, half:] * (B_DK ** -0.5)).astype(BF16)

    def finish_vb(r):
        vb_ref[...] = r.astype(BF16)

    def finish_gb(r):
        gb_ref[...] = (r * _sigmoid(r)).astype(BF16)

    def finish_qc(r):
        qc_ref[...] = (r * (C_HD ** -0.5)).astype(BF16)

    def finish_gates(r):
        gates_ref[...] = r.astype(BF16)

    finish = (finish_q, finish_k, finish_v, finish_qkb, finish_vb, finish_gb, finish_qc)

    for k in range(N_SEG):
        @pl.when(j == k)
        def _(k=k):
            finish[k](r)

    @pl.when(j >= N_SEG)
    def _():
        finish_gates(r)


def _inproj(x, g, w, tab, tm, seq_len=None):
    n, d = x.shape
    n_col = w.shape[1] // SEG
    n_gate = n_col - N_SEG
    tab_blocks = tab.shape[0] // tm
    token_minor = seq_len is not None
    rows = lambda dt: (jax.ShapeDtypeStruct((n, SEG), dt), pl.BlockSpec((tm, SEG), lambda i, j: (i, 0)))
    if token_minor:
        per_seq = seq_len // tm
        tmin = lambda dt: (jax.ShapeDtypeStruct((n // seq_len, SEG, seq_len), dt),
                           pl.BlockSpec((1, SEG, tm), lambda i, j: (i // per_seq, 0, i % per_seq)))
        outs = [tmin(BF16), tmin(F32), rows(BF16), rows(F32), tmin(BF16)]
    else:
        outs = [rows(BF16), rows(F32), rows(F32)]
    outs += [rows(BF16)] * 4
    outs.append((jax.ShapeDtypeStruct((n, n_gate * SEG), BF16),
                 pl.BlockSpec((tm, SEG), lambda i, j: (i, jnp.maximum(j - N_SEG, 0)))))
    out_shape = [o[0] for o in outs]
    out_specs = [o[1] for o in outs]
    return pl.pallas_call(
        functools.partial(_inproj_kernel, token_minor=token_minor),
        grid=(n // tm, n_col),
        in_specs=[
            pl.BlockSpec((tm, d), lambda i, j: (i, 0)),
            pl.BlockSpec((1, d), lambda i, j: (0, 0)),
            pl.BlockSpec((d, SEG), lambda i, j: (0, j)),
            pl.BlockSpec((tm, 6 * LANES), lambda i, j: (i % tab_blocks, 0)),
        ],
        out_specs=out_specs,
        out_shape=out_shape,
        scratch_shapes=[pltpu.VMEM((tm, d), BF16)],
        compiler_params=_cparams(("arbitrary", "arbitrary")),
        name="inproj",
    )(x, g, w, tab)


def _lambda(lq1_ref, lk1_ref, lq2_ref, lk2_ref, lam_init):
    a = jnp.sum(lq1_ref[...] * lk1_ref[...], axis=-1, keepdims=True)
    b = jnp.sum(lq2_ref[...] * lk2_ref[...], axis=-1, keepdims=True)
    return jnp.exp(a) - jnp.exp(b) + lam_init


def _subln(o, g, lam_init):
    y = o * lax.rsqrt(jnp.mean(o * o, axis=-1, keepdims=True) + EPS) * g
    return y * (1.0 - lam_init)


def _flash_kernel(lq1_ref, lk1_ref, lq2_ref, lk2_ref, gs_ref, qt_ref, k_ref, vt_ref, o_ref,
                  sa_ref, sb_ref, m_ref, l_ref, acc_ref, *, tq, lam_init):
    qi = pl.program_id(2)
    qt = qt_ref[0]
    feat = lax.broadcasted_iota(jnp.int32, qt.shape, 0)
    zero = jnp.zeros_like(qt)
    qs = jnp.concatenate([jnp.where(feat < A_HD, qt, zero), jnp.where(feat >= A_HD, qt, zero)], axis=1)

    s_refs = (sa_ref, sb_ref)

    def qk(kj, slot):
        start = pl.multiple_of(kj * tq, tq)
        s_refs[slot][...] = jnp.dot(k_ref[pl.ds(start, tq), :], qs, preferred_element_type=F32)

    def upd(kj, slot, diagonal):
        start = pl.multiple_of(kj * tq, tq)
        vb = vt_ref[0, :, pl.ds(start, tq)]
        s = s_refs[slot][...]
        if diagonal:
            key = lax.broadcasted_iota(jnp.int32, s.shape, 0)
            qry = lax.broadcasted_iota(jnp.int32, s.shape, 1) & (tq - 1)
            s = jnp.where(key <= qry, s, NEG_BIG)
        m = m_ref[...]
        m_new = jnp.maximum(m, jnp.max(s, axis=0, keepdims=True))
        corr = jnp.exp2(m - m_new)
        p = jnp.exp2(s - m_new)
        m_ref[...] = m_new
        l_ref[...] = l_ref[...] * corr + jnp.sum(p, axis=0, keepdims=True)
        acc_ref[...] = acc_ref[...] * corr + jnp.dot(vb, p.astype(BF16), preferred_element_type=F32)

    m_ref[...] = jnp.full(m_ref.shape, NEG_BIG, F32)
    l_ref[...] = jnp.zeros(l_ref.shape, F32)
    acc_ref[...] = jnp.zeros(acc_ref.shape, F32)
    qk(0, 0)

    def pair(pi, carry):
        b0 = 2 * pi
        qk(b0 + 1, 1)
        upd(b0, 0, False)
        qk(b0 + 2, 0)
        upd(b0 + 1, 1, False)
        return carry

    lax.fori_loop(0, qi // 2, pair, 0)

    @pl.when(qi % 2 == 1)
    def _():
        qk(qi, 1)
        upd(qi - 1, 0, False)
        upd(qi, 1, True)

    @pl.when(qi % 2 == 0)
    def _():
        upd(qi, 0, True)

    l = l_ref[...]
    acc = acc_ref[...]
    lam = _lambda(lq1_ref, lk1_ref, lq2_ref, lk2_ref, lam_init)
    o = acc[:, :tq] / l[:, :tq] - lam * (acc[:, tq:] / l[:, tq:])
    y = o * lax.rsqrt(jnp.mean(o * o, axis=0, keepdims=True) + EPS) * gs_ref[...] * (1.0 - lam_init)
    o_ref[...] = y.T.astype(BF16)


def _flash_diff_attn(lams, g_sub_col, qt, k, vt, b, t, tq, lam_init):
    assert tq & (tq - 1) == 0 and t % tq == 0
    nq = t // tq
    n = b * t
    vec = lambda shape: pl.BlockSpec(shape, lambda bi, h, qi: (0, 0))
    return pl.pallas_call(
        functools.partial(_flash_kernel, tq=tq, lam_init=lam_init),
        grid=(b, A_HEADS, nq),
        in_specs=[vec((1, A_HD))] * 4 + [
            vec((LANES, 1)),
            pl.BlockSpec((1, LANES, tq), lambda bi, h, qi: (bi, h, qi)),
            pl.BlockSpec((t, LANES), lambda bi, h, qi: (bi, h)),
            pl.BlockSpec((1, LANES, t), lambda bi, h, qi: (bi, h, 0)),
        ],
        out_specs=pl.BlockSpec((tq, LANES), lambda bi, h, qi: (bi * nq + qi, h)),
        out_shape=jax.ShapeDtypeStruct((n, A_HEADS * LANES), BF16),
        scratch_shapes=[pltpu.VMEM((tq, 2 * tq), F32), pltpu.VMEM((tq, 2 * tq), F32),
                        pltpu.VMEM((1, 2 * tq), F32), pltpu.VMEM((1, 2 * tq), F32),
                        pltpu.VMEM((LANES, 2 * tq), F32)],
        compiler_params=_cparams(("arbitrary", "arbitrary", "arbitrary")),
        name="flash_diff_attn",
    )(*lams, g_sub_col, qt, k, vt)


PAGED_ROWS = 2 * A_HEADS
PAGED_WIDTH = A_HEADS * LANES


def _paged_start(q_ref, kn_ref, vn_ref, qd_ref, m_ref, l_ref, acc_ref):
    rows, width = PAGED_ROWS, PAGED_WIDTH
    q = jnp.broadcast_to(q_ref[0].astype(F32), (rows, width))
    row = lax.broadcasted_iota(jnp.int32, (rows, width), 0)
    lane = lax.broadcasted_iota(jnp.int32, (rows, width), 1)
    qd = jnp.where(lane // A_HD == row, q, 0.0)
    qd_ref[...] = qd.astype(BF16)
    s_self = jnp.sum(qd * kn_ref[0], axis=-1, keepdims=True)
    m_ref[...] = jnp.broadcast_to(s_self, (rows, LANES))
    l_ref[...] = jnp.ones((rows, LANES), F32)
    acc_ref[...] = jnp.broadcast_to(vn_ref[0], (rows, width))


def _paged_scores(k_refs, qd_ref):
    qd = qd_ref[...]
    return jnp.concatenate([jnp.dot(qd, k_ref[0].astype(BF16), preferred_element_type=F32)
                            for k_ref in k_refs], axis=1)


def _paged_softmax(s, m_ref, l_ref):
    rows = PAGED_ROWS
    m = m_ref[:, 0:1]
    m_new = jnp.maximum(m, jnp.max(s, axis=-1, keepdims=True))
    corr = jnp.exp2(m - m_new)
    p = jnp.exp2(s - m_new)
    l = l_ref[:, 0:1] * corr + jnp.sum(p, axis=-1, keepdims=True)
    m_ref[...] = jnp.broadcast_to(m_new, (rows, LANES))
    l_ref[...] = jnp.broadcast_to(l, (rows, LANES))
    return p.astype(BF16), corr


def _paged_values(v_refs, pb, heads):
    page = v_refs[0].shape[1] // A_HEADS
    outs = []
    for h in heads:
        vh = jnp.concatenate([v_ref[0, pl.ds(h, page, stride=A_HEADS), :].astype(BF16)
                              for v_ref in v_refs], axis=0)
        outs.append(jnp.dot(pb, vh, preferred_element_type=F32))
    return outs


def _paged_accumulate(acc_ref, corr, outs, heads):
    for h, o in zip(heads, outs):
        cols = slice(h * LANES, (h + 1) * LANES)
        acc_ref[:, cols] = acc_ref[:, cols] * corr + o


def _paged_pages(k_refs, v_refs, qd_ref, m_ref, l_ref, acc_ref):
    pb, corr = _paged_softmax(_paged_scores(k_refs, qd_ref), m_ref, l_ref)
    heads = range(A_HEADS)
    _paged_accumulate(acc_ref, corr, _paged_values(v_refs, pb, heads), heads)


def _paged_finish(lam_refs, gs_ref, l_ref, acc_ref, o_ref, lam_init):
    lam = _lambda(*lam_refs, lam_init)
    g = gs_ref[...]
    outs = []
    for h in range(A_HEADS):
        cols = slice(h * LANES, (h + 1) * LANES)
        a0 = acc_ref[2 * h:2 * h + 1, cols] / l_ref[2 * h:2 * h + 1, 0:1]
        a1 = acc_ref[2 * h + 1:2 * h + 2, cols] / l_ref[2 * h + 1:2 * h + 2, 0:1]
        outs.append(_subln(a0 - lam * a1, g, lam_init))
    o_ref[0] = jnp.concatenate(outs, axis=1).astype(BF16)


def _paged_kernel(pt_ref, lq1_ref, lk1_ref, lq2_ref, lk2_ref, gs_ref, q_ref, kn_ref, vn_ref, *rest,
                  pages, lam_init):
    k_refs = rest[:pages]
    v_refs = rest[pages:2 * pages]
    o_ref = rest[2 * pages]
    state = rest[2 * pages + 1:]
    j = pl.program_id(1)

    @pl.when(j == 0)
    def _():
        _paged_start(q_ref, kn_ref, vn_ref, *state)

    _paged_pages(k_refs, v_refs, *state)

    @pl.when(j == pl.num_programs(1) - 1)
    def _():
        _paged_finish((lq1_ref, lk1_ref, lq2_ref, lk2_ref), gs_ref, state[2], state[3], o_ref, lam_init)


def _paged_specs(pages, width, page, seq_of, group_of):
    vec = lambda shape: pl.BlockSpec(shape, lambda a, b, pt: (0, 0))
    row3 = pl.BlockSpec((1, 1, width), lambda a, b, pt: (seq_of(a, b), 0, 0))

    def page_spec(r, shape):
        return pl.BlockSpec((1,) + shape,
                            lambda a, b, pt: (pt[seq_of(a, b), group_of(a, b) * pages + r], 0, 0))

    in_specs = ([vec((1, A_HD))] * 4 + [vec((1, LANES)), row3, row3, row3]
                + [page_spec(r, (width, page)) for r in range(pages)]
                + [page_spec(r, (page * A_HEADS, LANES)) for r in range(pages)])
    scratch = [pltpu.VMEM((PAGED_ROWS, width), BF16), pltpu.VMEM((PAGED_ROWS, LANES), F32),
               pltpu.VMEM((PAGED_ROWS, LANES), F32), pltpu.VMEM((PAGED_ROWS, width), F32)]
    return in_specs, row3, scratch


def _paged_diff_attn(page_table, lams, g_sub, q, k_new, v_new, cache_kt, cache_v, pages, lam_init):
    db, n_pages = page_table.shape
    width, page = cache_kt.shape[1:]
    assert n_pages % pages == 0 and cache_v.shape[1:] == (page * A_HEADS, LANES) and width == PAGED_WIDTH
    in_specs, out_spec, scratch = _paged_specs(pages, width, page, lambda a, b: a, lambda a, b: b)
    grid_spec = pltpu.PrefetchScalarGridSpec(
        num_scalar_prefetch=1, grid=(db, n_pages // pages),
        in_specs=in_specs, out_specs=out_spec, scratch_shapes=scratch)
    return pl.pallas_call(
        functools.partial(_paged_kernel, pages=pages, lam_init=lam_init),
        grid_spec=grid_spec,
        out_shape=jax.ShapeDtypeStruct((db, 1, width), BF16),
        compiler_params=_cparams(("arbitrary", "arbitrary")),
        name="paged_diff_attn",
    )(page_table, *lams, g_sub, q, k_new, v_new, *([cache_kt] * pages), *([cache_v] * pages))


def _head_mask(x, e):
    lane = lax.broadcasted_iota(jnp.int32, x.shape, x.ndim - 1)
    keep = (lane >= B_DK) if e else (lane < B_DK)
    return jnp.where(keep, x, jnp.zeros_like(x))


def _rms_plain(o):
    return o * lax.rsqrt(jnp.mean(o * o, axis=-1, keepdims=True) + EPS)


def _retention_kernel(qk_ref, v_ref, gs_ref, din_ref, dq_ref, dk_ref, dc_ref, o_ref, s_out_ref, s_ref, *, c):
    ci = pl.program_id(1)

    @pl.when(ci == 0)
    def _():
        s_ref[...] = jnp.zeros_like(s_ref)

    half = B_HEADS * B_DK
    n_sub = qk_ref.shape[0] // c
    for p in range(B_HEADS // 2):
        state = s_ref[p]
        for sub in range(n_sub):
            tok = slice(sub * c, (sub + 1) * c)
            q2 = qk_ref[tok, p * LANES:(p + 1) * LANES]
            k2 = qk_ref[tok, half + p * LANES:half + (p + 1) * LANES]
            s_bf = state.astype(BF16)
            kd = (k2.astype(F32) * dk_ref[p]).astype(BF16)
            new_rows = []
            for e in range(2):
                h = 2 * p + e
                qz = _head_mask(q2, e)
                vh = v_ref[tok, h * LANES:(h + 1) * LANES]
                attn = lax.dot_general(qz, k2, NT_DIMS, preferred_element_type=F32) * din_ref[h]
                o = (jnp.dot(attn.astype(BF16), vh, preferred_element_type=F32)
                     + jnp.dot(qz, s_bf, preferred_element_type=F32) * dq_ref[h])
                u = lax.dot_general(kd, vh, TN_DIMS, preferred_element_type=F32)
                rows = slice(e * B_DK, (e + 1) * B_DK)
                new_rows.append(state[rows] * dc_ref[p][rows] + u[rows])
                gate = gs_ref[tok, h * LANES:(h + 1) * LANES].astype(F32)
                o_ref[tok, h * LANES:(h + 1) * LANES] = (_rms_plain(o) * gate).astype(BF16)
            state = jnp.concatenate(new_rows, axis=0)
        s_ref[p] = state

    @pl.when(ci == pl.num_programs(1) - 1)
    def _():
        s_out_ref[0] = s_ref[...].reshape(B_HEADS, B_DK, B_DV)


def _retention_prompt(qk, v, gs, b, t):
    c = math.gcd(t, RET_CHUNK)
    blk = math.gcd(t, RET_CHUNKS_PER_STEP * c)
    nc = t // blk
    n = b * t
    d_in, d_q, d_k, d_c = _retention_tables(c)
    rows = lambda bi, ci: (bi * nc + ci, 0)
    const3 = lambda bi, ci: (0, 0, 0)
    width = B_HEADS * B_DV
    return pl.pallas_call(
        functools.partial(_retention_kernel, c=c),
        grid=(b, nc),
        in_specs=[
            pl.BlockSpec((blk, 2 * B_HEADS * B_DK), rows),
            pl.BlockSpec((blk, width), rows),
            pl.BlockSpec((blk, width), rows),
            pl.BlockSpec(d_in.shape, const3),
            pl.BlockSpec(d_q.shape, const3),
            pl.BlockSpec(d_k.shape, const3),
            pl.BlockSpec(d_c.shape, const3),
        ],
        out_specs=[
            pl.BlockSpec((blk, width), rows),
            pl.BlockSpec((1, B_HEADS, B_DK, B_DV), lambda bi, ci: (bi, 0, 0, 0)),
        ],
        out_shape=[jax.ShapeDtypeStruct((n, width), BF16),
                   jax.ShapeDtypeStruct((b, B_HEADS, B_DK, B_DV), F32)],
        scratch_shapes=[pltpu.VMEM((B_HEADS // 2, 2 * B_DK, B_DV), F32)],
        compiler_params=_cparams(("arbitrary", "arbitrary")),
        name="retention_prompt",
    )(qk, v, gs, d_in, d_q, d_k, d_c)


def _retention_step_kernel(qk_ref, v_ref, gs_ref, s0_ref, dq_ref, dc_ref, o_ref, s_out_ref):
    half = B_HEADS * B_DK
    pad = BF16_SUBLANES
    row0 = lax.broadcasted_iota(jnp.int32, (pad, LANES), 0) == 0
    qk = qk_ref[0].astype(F32)
    v = v_ref[0].astype(F32)

    def pad_rows(x):
        return jnp.where(row0, jnp.broadcast_to(x, (pad, LANES)), 0.0).astype(BF16)

    for p in range(B_HEADS // 2):
        q2 = qk[:, p * LANES:(p + 1) * LANES]
        k2 = qk[:, half + p * LANES:half + (p + 1) * LANES]
        s_old = s0_ref[0, 2 * p:2 * p + 2].reshape(2 * B_DK, B_DV)
        s_bf = s_old.astype(BF16)
        k8 = pad_rows(k2)
        new_rows = []
        for e in range(2):
            h = 2 * p + e
            qz = _head_mask(q2, e)
            vh = v[:, h * LANES:(h + 1) * LANES]
            qk_dot = jnp.sum(qz * k2, axis=-1, keepdims=True)
            qs = jnp.dot(pad_rows(qz), s_bf, preferred_element_type=F32)[0:1]
            o = qk_dot * vh + qs * dq_ref[h][0:1]
            u = lax.dot_general(k8, pad_rows(vh), TN_DIMS, preferred_element_type=F32)
            rows = slice(e * B_DK, (e + 1) * B_DK)
            new_rows.append(s_old[rows] * dc_ref[p][rows] + u[rows])
            gate = gs_ref[0][:, h * LANES:(h + 1) * LANES].astype(F32)
            o_ref[0, :, h * LANES:(h + 1) * LANES] = (_rms_plain(o) * gate).astype(BF16)
        s_out_ref[0, 2 * p:2 * p + 2] = jnp.concatenate(new_rows, axis=0).reshape(2, B_DK, B_DV)


def _retention_sample(qk, v, gs, s0):
    db = qk.shape[0]
    _, d_q, _, d_c = _retention_tables(1)
    width = B_HEADS * B_DV
    row3 = lambda w: pl.BlockSpec((1, 1, w), lambda bi: (bi, 0, 0))
    state = pl.BlockSpec((1, B_HEADS, B_DK, B_DV), lambda bi: (bi, 0, 0, 0))
    const3 = lambda bi: (0, 0, 0)
    return pl.pallas_call(
        _retention_step_kernel,
        grid=(db,),
        in_specs=[row3(2 * B_HEADS * B_DK), row3(width), row3(width), state,
                  pl.BlockSpec(d_q.shape, const3), pl.BlockSpec(d_c.shape, const3)],
        out_specs=[row3(width), state],
        out_shape=[jax.ShapeDtypeStruct((db, 1, width), BF16),
                   jax.ShapeDtypeStruct(s0.shape, F32)],
        compiler_params=_cparams(("arbitrary",)),
        name="retention_sample",
    )(qk, v, gs, s0, d_q, d_c)


def _rms_matmul_kernel(x_ref, g_ref, w_ref, of_ref, ob_ref, xn_ref):
    @pl.when(pl.program_id(1) == 0)
    def _():
        x = x_ref[...]
        xn = x * lax.rsqrt(jnp.mean(x * x, axis=-1, keepdims=True) + EPS) * g_ref[...]
        xn_ref[...] = xn.astype(BF16)

    r = jnp.dot(xn_ref[...], w_ref[...], preferred_element_type=F32)
    of_ref[...] = r
    ob_ref[...] = r.astype(BF16)


def _rms_matmul(x, g, w, tm, tn):
    n, d = x.shape
    width = w.shape[1]
    return pl.pallas_call(
        _rms_matmul_kernel,
        grid=(n // tm, width // tn),
        in_specs=[pl.BlockSpec((tm, d), lambda i, j: (i, 0)),
                  pl.BlockSpec((1, d), lambda i, j: (0, 0)),
                  pl.BlockSpec((d, tn), lambda i, j: (0, j))],
        out_specs=[pl.BlockSpec((tm, tn), lambda i, j: (i, j))] * 2,
        out_shape=[jax.ShapeDtypeStruct((n, width), F32), jax.ShapeDtypeStruct((n, width), BF16)],
        scratch_shapes=[pltpu.VMEM((tm, d), BF16)],
        compiler_params=_cparams(("arbitrary", "arbitrary")),
        name="mem_kv",
    )(x, g, w)


def _softmax_rows(s):
    m = jnp.max(s, axis=-1, keepdims=True)
    p = jnp.exp(s - m)
    return p / jnp.sum(p, axis=-1, keepdims=True)


def _mem_attn_kernel(q_ref, mk_ref, mv_ref, o_ref):
    for h in range(C_HEADS):
        cols = slice(h * C_HD, (h + 1) * C_HD)
        s = lax.dot_general(q_ref[:, cols], mk_ref[:, cols], NT_DIMS, preferred_element_type=F32)
        p = _softmax_rows(s)
        o_ref[:, cols] = jnp.dot(p.astype(BF16), mv_ref[:, cols], preferred_element_type=F32).astype(BF16)


def _mem_attn_prompt(q, mk, mv, t, tm):
    n, width = q.shape
    mem_len = mk.shape[0] // (n // t)
    per_seq = t // tm
    return pl.pallas_call(
        _mem_attn_kernel,
        grid=(n // tm,),
        in_specs=[pl.BlockSpec((tm, width), lambda i: (i, 0)),
                  pl.BlockSpec((mem_len, width), lambda i: (i // per_seq, 0)),
                  pl.BlockSpec((mem_len, width), lambda i: (i // per_seq, 0))],
        out_specs=pl.BlockSpec((tm, width), lambda i: (i, 0)),
        out_shape=jax.ShapeDtypeStruct((n, width), BF16),
        compiler_params=_cparams(("arbitrary",)),
        name="mem_attn_prompt",
    )(q, mk, mv)


def _mem_attn_step_kernel(q_ref, mk_ref, mv_ref, o_ref):
    pad = BF16_SUBLANES
    width = C_HEADS * C_HD
    q = jnp.broadcast_to(q_ref[0].astype(F32), (pad, width))
    row = lax.broadcasted_iota(jnp.int32, (pad, width), 0)
    lane = lax.broadcasted_iota(jnp.int32, (pad, width), 1)
    qd = jnp.where(lane // C_HD == row, q, 0.0).astype(BF16)
    s = lax.dot_general(qd, mk_ref[0].astype(BF16), NT_DIMS, preferred_element_type=F32)
    p = _softmax_rows(s)
    o = jnp.dot(p.astype(BF16), mv_ref[0].astype(BF16), preferred_element_type=F32)
    o_ref[0] = jnp.concatenate([o[h:h + 1, h * C_HD:(h + 1) * C_HD] for h in range(C_HEADS)],
                               axis=1).astype(BF16)


def _mem_attn_sample(q, mk, mv):
    db, mem_len, width = mk.shape
    row3 = pl.BlockSpec((1, 1, width), lambda bi: (bi, 0, 0))
    mem = pl.BlockSpec((1, mem_len, width), lambda bi: (bi, 0, 0))
    return pl.pallas_call(
        _mem_attn_step_kernel,
        grid=(db,),
        in_specs=[row3, mem, mem],
        out_specs=row3,
        out_shape=jax.ShapeDtypeStruct((db, 1, width), BF16),
        compiler_params=_cparams(("arbitrary",)),
        name="mem_attn_sample",
    )(q, mk, mv)


def _merge_kernel(x_ref, oa_ref, ob_ref, oc_ref, gates_ref, pa_ref, pb_ref, pc_ref, wo_ref, h_ref):
    d = x_ref.shape[1]
    gate = lambda k: _sigmoid(gates_ref[:, k * d:(k + 1) * d].astype(F32))
    merged = (gate(0) * jnp.dot(oa_ref[...], pa_ref[...], preferred_element_type=F32)
              + gate(1) * jnp.dot(ob_ref[...], pb_ref[...], preferred_element_type=F32)
              + gate(2) * jnp.dot(oc_ref[...], pc_ref[...], preferred_element_type=F32))
    h_ref[...] = x_ref[...] + jnp.dot(merged.astype(BF16), wo_ref[...], preferred_element_type=F32)


def _merge(x, oa, ob, oc, gates, p_a, p_b, p_c, w_o, tm):
    n, d = x.shape
    rows = lambda w: pl.BlockSpec((tm, w), lambda i: (i, 0))
    whole = lambda a: _resident(a.shape, lambda i: (0, 0))
    return pl.pallas_call(
        _merge_kernel,
        grid=(n // tm,),
        in_specs=[rows(d), rows(oa.shape[1]), rows(ob.shape[1]), rows(oc.shape[1]), rows(gates.shape[1]),
                  whole(p_a), whole(p_b), whole(p_c), whole(w_o)],
        out_specs=rows(d),
        out_shape=jax.ShapeDtypeStruct((n, d), F32),
        compiler_params=_cparams(("arbitrary",)),
        name="merge",
    )(x, oa, ob, oc, gates, p_a, p_b, p_c, w_o)


def _mlp_kernel(h_ref, g_ref, wu_ref, wd_ref, gf_ref, y_ref, hn_ref, acc_ref):
    f = pl.program_id(1)

    @pl.when(f == 0)
    def _():
        h = h_ref[...]
        hn = h * lax.rsqrt(jnp.mean(h * h, axis=-1, keepdims=True) + EPS) * g_ref[...]
        hn_ref[...] = hn.astype(BF16)
        acc_ref[...] = jnp.zeros_like(acc_ref)

    u = jnp.dot(hn_ref[...], wu_ref[...], preferred_element_type=F32)
    a = jnp.square(jnp.maximum(u, 0.0)).astype(BF16)
    acc_ref[...] += jnp.dot(a, wd_ref[...], preferred_element_type=F32)

    @pl.when(f == pl.num_programs(1) - 1)
    def _():
        h2 = h_ref[...] + acc_ref[...]
        y_ref[...] = h2 * lax.rsqrt(jnp.mean(h2 * h2, axis=-1, keepdims=True) + EPS) * gf_ref[...]


def _mlp_paged_kernel(pt_ref, h_ref, g_ref, wu_ref, wd_ref, gf_ref, lq1_ref, lk1_ref, lq2_ref, lk2_ref,
                      gs_ref, q_ref, kn_ref, vn_ref, *rest, pages, lam_init):
    k_refs = rest[:pages]
    v_refs = rest[pages:2 * pages]
    y_ref, o_ref, hn_ref, acc_ref = rest[2 * pages:2 * pages + 4]
    state = rest[2 * pages + 4:]
    f = pl.program_id(1)
    last = pl.num_programs(1) - 1

    @pl.when(f == 0)
    def _():
        h = h_ref[...]
        hn = h * lax.rsqrt(jnp.mean(h * h, axis=-1, keepdims=True) + EPS) * g_ref[...]
        hn_ref[...] = hn.astype(BF16)
        acc_ref[...] = jnp.zeros_like(acc_ref)
        _paged_start(q_ref, kn_ref, vn_ref, *state)

    qd_ref, pm_ref, pl_ref, pacc_ref = state
    s = _paged_scores(k_refs, qd_ref)
    u = jnp.dot(hn_ref[...], wu_ref[...], preferred_element_type=F32)
    pb, corr = _paged_softmax(s, pm_ref, pl_ref)
    first, second = range(A_HEADS // 2), range(A_HEADS // 2, A_HEADS)
    o_first = _paged_values(v_refs, pb, first)
    a = jnp.square(jnp.maximum(u, 0.0)).astype(BF16)
    o_second = _paged_values(v_refs, pb, second)
    acc_ref[...] += jnp.dot(a, wd_ref[...], preferred_element_type=F32)
    _paged_accumulate(pacc_ref, corr, o_first, first)
    _paged_accumulate(pacc_ref, corr, o_second, second)

    @pl.when(f == last)
    def _():
        h2 = h_ref[...] + acc_ref[...]
        y_ref[...] = h2 * lax.rsqrt(jnp.mean(h2 * h2, axis=-1, keepdims=True) + EPS) * gf_ref[...]
        _paged_finish((lq1_ref, lk1_ref, lq2_ref, lk2_ref), gs_ref, state[2], state[3], o_ref, lam_init)


def _mlp_with_paged(h, g_mlp, w_up, w_down, g_final, tm, tf,
                    page_table, lams, g_sub, q, k_new, v_new, cache_kt, cache_v, lam_init):
    n, d = h.shape
    d_ff = w_up.shape[1]
    db, n_pages = page_table.shape
    n_f = d_ff // tf
    width, page = cache_kt.shape[1:]
    assert n // tm == db and n_pages % n_f == 0
    pages = n_pages // n_f
    paged_in, paged_out, paged_scratch = _paged_specs(pages, width, page, lambda i, f: i, lambda i, f: f)
    grid_spec = pltpu.PrefetchScalarGridSpec(
        num_scalar_prefetch=1,
        grid=(n // tm, n_f),
        in_specs=[pl.BlockSpec((tm, d), lambda i, f, pt: (i, 0)),
                  pl.BlockSpec((1, d), lambda i, f, pt: (0, 0)),
                  pl.BlockSpec((d, tf), lambda i, f, pt: (0, f)),
                  pl.BlockSpec((tf, d), lambda i, f, pt: (f, 0)),
                  pl.BlockSpec((1, d), lambda i, f, pt: (0, 0))] + paged_in,
        out_specs=[pl.BlockSpec((tm, d), lambda i, f, pt: (i, 0)), paged_out],
        scratch_shapes=[pltpu.VMEM((tm, d), BF16), pltpu.VMEM((tm, d), F32)] + paged_scratch,
    )
    return pl.pallas_call(
        functools.partial(_mlp_paged_kernel, pages=pages, lam_init=lam_init),
        grid_spec=grid_spec,
        out_shape=[jax.ShapeDtypeStruct((n, d), F32), jax.ShapeDtypeStruct((db, 1, width), BF16)],
        compiler_params=_cparams(("arbitrary", "arbitrary")),
        name="mlp_paged",
    )(page_table, h, g_mlp, w_up, w_down, g_final, *lams, g_sub, q, k_new, v_new,
      *([cache_kt] * pages), *([cache_v] * pages))


def _mlp(h, g_mlp, w_up, w_down, g_final, tm, tf):
    n, d = h.shape
    d_ff = w_up.shape[1]
    return pl.pallas_call(
        _mlp_kernel,
        grid=(n // tm, d_ff // tf),
        in_specs=[pl.BlockSpec((tm, d), lambda i, f: (i, 0)),
                  pl.BlockSpec((1, d), lambda i, f: (0, 0)),
                  pl.BlockSpec((d, tf), lambda i, f: (0, f)),
                  pl.BlockSpec((tf, d), lambda i, f: (f, 0)),
                  pl.BlockSpec((1, d), lambda i, f: (0, 0))],
        out_specs=pl.BlockSpec((tm, d), lambda i, f: (i, 0)),
        out_shape=jax.ShapeDtypeStruct((n, d), F32),
        scratch_shapes=[pltpu.VMEM((tm, d), BF16), pltpu.VMEM((tm, d), F32)],
        compiler_params=_cparams(("arbitrary", "arbitrary")),
        name="mlp",
    )(h, g_mlp, w_up, w_down, g_final)


class _Tiles(NamedTuple):
    rows: int
    merge_rows: int
    flash: int
    memkv_rows: int
    memkv_cols: int
    mlp_ff: int
    mlp_ff_fused: int
    pages: int


def _tiles(t, n_mem):
    return _Tiles(rows=min(t, 512), merge_rows=min(t, 256), flash=min(t, 512),
                  memkv_rows=min(n_mem, 256), memkv_cols=1024, mlp_ff=1024, mlp_ff_fused=512, pages=8)


def kernel(x_prompt, x_sample, cache_diff_k, cache_diff_v, state_ret, cache_mem_k, cache_mem_v, page_table, mem_prompt, g_mix, w_in, lam_q1, lam_k1, lam_q2, lam_k2, g_subln, g_mem, w_mem_k, w_mem_v, p_a, p_b, p_c, w_o, g_mlp, w_up, w_down, g_final):
    depth = g_mix.shape[0]
    assert depth == 1
    layer = 0
    lam_init = 0.8 - 0.6 * math.exp(-0.3 * layer)
    b, t, d = x_prompt.shape
    db, dt, _ = x_sample.shape
    assert dt == 1
    n_pool, page = cache_diff_k.shape[1:3]
    past_len = page_table.shape[1] * page
    mem_len = mem_prompt.shape[1]
    n_p, n_s = b * t, db * dt

    row = lambda a: a[layer].reshape(1, -1)
    lams = (row(lam_q1), row(lam_k1), row(lam_q2), row(lam_k2))
    g_sub = row(g_subln)
    w_in_b = w_in[layer].astype(BF16)
    w_mem_b = jnp.concatenate([w_mem_k[layer], w_mem_v[layer]], axis=1).astype(BF16)
    pa_b, pb_b, pc_b, wo_b = (a[layer].astype(BF16) for a in (p_a, p_b, p_c, w_o))
    wu_b, wd_b = w_up[layer].astype(BF16), w_down[layer].astype(BF16)
    g_final2 = g_final.reshape(1, -1)

    tiles = _tiles(t, b * mem_len)
    tm_p = tiles.rows
    xp = x_prompt.reshape(n_p, d)
    tab_p = _rope_lane_tables(jnp.arange(t))
    qt, ktf, kab, vaf, vtb, qkb, vb, gsb, qc, gates = _inproj(xp, row(g_mix), w_in_b, tab_p, tm_p, seq_len=t)
    oa = _flash_diff_attn(lams, g_sub.reshape(-1, 1), qt, kab, vtb, b, t, tiles.flash, lam_init)
    ob, state_p = _retention_prompt(qkb, vb, gsb, b, t)
    mem_f, mem_b = _rms_matmul(mem_prompt.reshape(b * mem_len, d), row(g_mem), w_mem_b,
                               tiles.memkv_rows, tiles.memkv_cols)
    c_w = C_HEADS * C_HD
    oc = _mem_attn_prompt(qc, mem_b[:, :c_w], mem_b[:, c_w:], t, tm_p)
    hp = _merge(xp, oa, ob, oc, gates, pa_b, pb_b, pc_b, wo_b, tiles.merge_rows)

    xs = x_sample.reshape(n_s, d)
    tab_s = jnp.tile(_rope_lane_tables(past_len + jnp.arange(dt)), (db, 1))
    qa_s, kaf_s, vaf_s, qkb_s, vb_s, gsb_s, qc_s, gates_s = _inproj(xs, row(g_mix), w_in_b, tab_s, n_s)
    width_a = A_HEADS * LANES
    r3 = lambda a: a.reshape(db, 1, a.shape[-1])
    ckt = cache_diff_k[layer].transpose(0, 2, 3, 4, 1).reshape(n_pool, width_a, page)
    cv = cache_diff_v[layer].reshape(n_pool, page * A_HEADS, LANES)
    paged_args = (page_table, lams, g_sub, r3(qa_s), r3(kaf_s), r3(vaf_s), ckt, cv)
    d_ff = wu_b.shape[1]
    if n_p // tm_p == db and page_table.shape[1] % (d_ff // tiles.mlp_ff_fused) == 0:
        y_p, oa_s = _mlp_with_paged(hp, row(g_mlp), wu_b, wd_b, g_final2, tm_p, tiles.mlp_ff_fused,
                                    *paged_args, lam_init)
    else:
        y_p = _mlp(hp, row(g_mlp), wu_b, wd_b, g_final2, tm_p, tiles.mlp_ff)
        oa_s = _paged_diff_attn(*paged_args, tiles.pages, lam_init)
    ob_s, state_s = _retention_sample(r3(qkb_s), r3(vb_s), r3(gsb_s), state_ret[layer])
    oc_s = _mem_attn_sample(r3(qc_s), cache_mem_k[layer].reshape(db, mem_len, c_w),
                            cache_mem_v[layer].reshape(db, mem_len, c_w))
    hs = _merge(xs, oa_s.reshape(n_s, -1), ob_s.reshape(n_s, -1), oc_s.reshape(n_s, -1), gates_s,
                pa_b, pb_b, pc_b, wo_b, n_s)
    y_s = _mlp(hs, row(g_mlp), wu_b, wd_b, g_final2, n_s, tiles.mlp_ff)

    return (
        y_p.reshape(b, t, d),
        y_s.reshape(db, dt, d),
        ktf.reshape(1, b, A_HEADS, 2, A_HD, t).transpose(0, 1, 5, 2, 3, 4),
        vaf.reshape(1, b, t, A_HEADS, 2 * A_HD),
        state_p[None],
        mem_f[:, :c_w].reshape(1, b, mem_len, C_HEADS, C_HD),
        mem_f[:, c_w:].reshape(1, b, mem_len, C_HEADS, C_HD),
        kaf_s.reshape(1, db, dt, A_HEADS, 2, A_HD),
        vaf_s.reshape(1, db, dt, A_HEADS, 2 * A_HD),
        state_s[None],
    )
```

```python
import functools
import math
from typing import NamedTuple

import jax
import jax.numpy as jnp
from jax import lax
from jax.experimental import pallas as pl
from jax.experimental.pallas import tpu as pltpu

F32 = jnp.float32
BF16 = jnp.bfloat16
EPS = 1e-6
NEG_BIG = -1e30
LOG2E = math.log2(math.e)

LANES = 128
BF16_SUBLANES = 16
VMEM_LIMIT = 56 * 1024 * 1024

A_HEADS = 8
A_HD = 64
A_ROT = A_HD // 4
ROPE_THETA = 500000.0
B_HEADS = 8
B_DK = 64
B_DV = 128
RET_THETA = 10000.0
RET_CHUNK = 128
RET_CHUNKS_PER_STEP = 4
C_HEADS = 4
C_HD = 256
SEG = 1024
N_SEG = 7

NT_DIMS = (((1,), (1,)), ((), ()))
TN_DIMS = (((0,), (0,)), ((), ()))


def _cparams(sem):
    return pltpu.CompilerParams(dimension_semantics=sem, vmem_limit_bytes=VMEM_LIMIT)


def _resident(shape, index_map):
    return pl.BlockSpec(shape, index_map, pipeline_mode=pl.Buffered(1))


def _rope_angles(pos, n_rot, theta):
    inv = 1.0 / (theta ** (jnp.arange(0, n_rot, 2, dtype=F32) / n_rot))
    ang = pos.astype(F32)[:, None] * inv[None, :]
    return jnp.cos(ang), jnp.sin(ang)


def _rope_lane_tables(pos):
    t = pos.shape[0]
    ca, sa = _rope_angles(pos, A_ROT, ROPE_THETA)
    ha = A_ROT // 2
    c_a = jnp.concatenate([ca, ca, jnp.ones((t, A_HD - A_ROT), F32)], axis=1)
    p_a = jnp.concatenate([jnp.zeros((t, ha), F32), sa, jnp.zeros((t, A_HD - A_ROT), F32)], axis=1)
    n_a = jnp.concatenate([-sa, jnp.zeros((t, A_HD - ha), F32)], axis=1)
    cb, sb = _rope_angles(pos, B_DK, RET_THETA)
    hb = B_DK // 2
    c_b = jnp.concatenate([cb, cb], axis=1)
    p_b = jnp.concatenate([jnp.zeros((t, hb), F32), sb], axis=1)
    n_b = jnp.concatenate([-sb, jnp.zeros((t, hb), F32)], axis=1)
    return jnp.concatenate([jnp.tile(a, (1, 2)) for a in (c_a, p_a, n_a, c_b, p_b, n_b)], axis=1)


def _retention_tables(c):
    log_g = jnp.log(1.0 - 2.0 ** (-5.0 - jnp.arange(B_HEADS, dtype=F32)))
    idx = jnp.arange(c, dtype=F32)
    rel = idx[:, None] - idx[None, :]
    d_in = jnp.where(rel >= 0, jnp.exp(log_g[:, None, None] * jnp.maximum(rel, 0.0)), 0.0)
    d_q = jnp.exp(log_g[:, None] * (idx + 1.0))
    d_k = jnp.exp(log_g[:, None] * (c - 1.0 - idx))
    d_c = jnp.exp(log_g * c)
    d_q = jnp.broadcast_to(d_q[:, :, None], (B_HEADS, c, LANES))
    d_k = jnp.broadcast_to(d_k[:, :, None], (B_HEADS, c, B_DK)).reshape(B_HEADS // 2, 2, c, B_DK)
    d_k = d_k.transpose(0, 2, 1, 3).reshape(B_HEADS // 2, c, 2 * B_DK)
    d_c = jnp.broadcast_to(d_c[:, None, None], (B_HEADS, B_DK, LANES)).reshape(B_HEADS // 2, 2 * B_DK, LANES)
    return d_in, d_q, d_k, d_c


def _rope_cols(r, c, p, n, half):
    outs = []
    for h in range(r.shape[1] // LANES):
        blk = r[:, h * LANES:(h + 1) * LANES]
        outs.append(blk * c + pltpu.roll(blk, half, 1) * p + pltpu.roll(blk, LANES - half, 1) * n)
    return jnp.concatenate(outs, axis=1)


def _rope_rows(xt, c, s):
    half = A_ROT // 2
    pieces = []
    for base in range(0, xt.shape[0], A_HD):
        x1 = xt[base:base + half]
        x2 = xt[base + half:base + 2 * half]
        pieces += [x1 * c - x2 * s, x2 * c + x1 * s, xt[base + 2 * half:base + A_HD]]
    return jnp.concatenate(pieces, axis=0)


def _sigmoid(x):
    return 1.0 / (1.0 + jnp.exp(-x))


def _inproj_kernel(x_ref, g_ref, w_ref, tab_ref, *refs, token_minor):
    xn_ref = refs[-1]
    if token_minor:
        tabt_ref = refs[0]
        qa_ref, kaf_ref, kab_ref, vaf_ref, vtb_ref, qkb_ref, vb_ref, gb_ref, qc_ref, gates_ref = refs[1:-1]
    else:
        qa_ref, kaf_ref, vaf_ref, qkb_ref, vb_ref, gb_ref, qc_ref, gates_ref = refs[:-1]
    j = pl.program_id(1)

    @pl.when(j == 0)
    def _():
        x = x_ref[...]
        xn = x * lax.rsqrt(jnp.mean(x * x, axis=-1, keepdims=True) + EPS) * g_ref[...]
        xn_ref[...] = xn.astype(BF16)

    r = jnp.dot(xn_ref[...], w_ref[...], preferred_element_type=F32)

    def rope_a(v):
        return _rope_cols(v, tab_ref[:, 0:LANES], tab_ref[:, LANES:2 * LANES],
                          tab_ref[:, 2 * LANES:3 * LANES], A_ROT // 2)

    def rope_b(v):
        return _rope_cols(v, tab_ref[:, 3 * LANES:4 * LANES], tab_ref[:, 4 * LANES:5 * LANES],
                          tab_ref[:, 5 * LANES:6 * LANES], B_DK // 2)

    def finish_q(r):
        scale = A_HD ** -0.5 * LOG2E
        if token_minor:
            half = A_ROT // 2
            qa_ref[0] = _rope_rows((r * scale).T, tabt_ref[0:half, :], tabt_ref[half:2 * half, :]).astype(BF16)
        else:
            qa_ref[...] = (rope_a(r) * scale).astype(BF16)

    def finish_k(r):
        if token_minor:
            half = A_ROT // 2
            kt = _rope_rows(r.T, tabt_ref[0:half, :], tabt_ref[half:2 * half, :])
            kaf_ref[0] = kt
            kab_ref[...] = kt.T.astype(BF16)
        else:
            kaf_ref[...] = rope_a(r)

    def finish_v(r):
        vaf_ref[...] = r
        if token_minor:
            vtb_ref[0] = r.T.astype(BF16)

    def finish_qkb(r):
        rb = rope_b(r)
        half = SEG // 2
        qkb_ref[:, :half] = rb[:, :half].astype(BF16)
        qkb_ref[:, half:] = (rb[:, half:] * (B_DK ** -0.5)).astype(BF16)

    def finish_vb(r):
        vb_ref[...] = r.astype(BF16)

    def finish_gb(r):
        gb_ref[...] = (r * _sigmoid(r)).astype(BF16)

    def finish_qc(r):
        qc_ref[...] = (r * (C_HD ** -0.5)).astype(BF16)

    def finish_gates(r):
        gates_ref[...] = r.astype(BF16)

    finish = (finish_q, finish_k, finish_v, finish_qkb, finish_vb, finish_gb, finish_qc)

    for k in range(N_SEG):
        @pl.when(j == k)
        def _(k=k):
            finish[k](r)

    @pl.when(j >= N_SEG)
    def _():
        finish_gates(r)


def _inproj(x, g, w, tab, tm, seq_len=None, tab_t=None):
    n, d = x.shape
    n_col = w.shape[1] // SEG
    n_gate = n_col - N_SEG
    tab_blocks = tab.shape[0] // tm
    token_minor = seq_len is not None
    rows = lambda dt: (jax.ShapeDtypeStruct((n, SEG), dt), pl.BlockSpec((tm, SEG), lambda i, j: (i, 0)))
    extra_in, extra_specs = (), []
    if token_minor:
        extra_in = (tab_t,)
        extra_specs = [pl.BlockSpec((A_ROT, tm), lambda i, j: (0, i % tab_blocks))]
        per_seq = seq_len // tm
        tmin = lambda dt: (jax.ShapeDtypeStruct((n // seq_len, SEG, seq_len), dt),
                           pl.BlockSpec((1, SEG, tm), lambda i, j: (i // per_seq, 0, i % per_seq)))
        outs = [tmin(BF16), tmin(F32), rows(BF16), rows(F32), tmin(BF16)]
    else:
        outs = [rows(BF16), rows(F32), rows(F32)]
    outs += [rows(BF16)] * 4
    outs.append((jax.ShapeDtypeStruct((n, n_gate * SEG), BF16),
                 pl.BlockSpec((tm, SEG), lambda i, j: (i, jnp.maximum(j - N_SEG, 0)))))
    out_shape = [o[0] for o in outs]
    out_specs = [o[1] for o in outs]
    return pl.pallas_call(
        functools.partial(_inproj_kernel, token_minor=token_minor),
        grid=(n // tm, n_col),
        in_specs=[
            pl.BlockSpec((tm, d), lambda i, j: (i, 0)),
            pl.BlockSpec((1, d), lambda i, j: (0, 0)),
            pl.BlockSpec((d, SEG), lambda i, j: (0, j)),
            pl.BlockSpec((tm, 6 * LANES), lambda i, j: (i % tab_blocks, 0)),
        ] + extra_specs,
        out_specs=out_specs,
        out_shape=out_shape,
        scratch_shapes=[pltpu.VMEM((tm, d), BF16)],
        compiler_params=_cparams(("arbitrary", "arbitrary")),
        name="inproj",
    )(x, g, w, tab, *extra_in)


def _lambda(lq1_ref, lk1_ref, lq2_ref, lk2_ref, lam_init):
    a = jnp.sum(lq1_ref[...] * lk1_ref[...], axis=-1, keepdims=True)
    b = jnp.sum(lq2_ref[...] * lk2_ref[...], axis=-1, keepdims=True)
    return jnp.exp(a) - jnp.exp(b) + lam_init


def _subln(o, g, lam_init):
    y = o * lax.rsqrt(jnp.mean(o * o, axis=-1, keepdims=True) + EPS) * g
    return y * (1.0 - lam_init)


def _flash_kernel(lq1_ref, lk1_ref, lq2_ref, lk2_ref, gs_ref, qt_ref, k_ref, vt_ref, o_ref,
                  sa_ref, sb_ref, m_ref, l_ref, acc_ref, *, tq, lam_init):
    qi = pl.program_id(2)
    qt = qt_ref[0]
    feat = lax.broadcasted_iota(jnp.int32, qt.shape, 0)
    zero = jnp.zeros_like(qt)
    qs = jnp.concatenate([jnp.where(feat < A_HD, qt, zero), jnp.where(feat >= A_HD, qt, zero)], axis=1)

    s_refs = (sa_ref, sb_ref)

    def qk(kj, slot):
        start = pl.multiple_of(kj * tq, tq)
        s_refs[slot][...] = jnp.dot(k_ref[pl.ds(start, tq), :], qs, preferred_element_type=F32)

    def upd(kj, slot, diagonal):
        start = pl.multiple_of(kj * tq, tq)
        vb = vt_ref[0, :, pl.ds(start, tq)]
        s = s_refs[slot][...]
        if diagonal:
            key = lax.broadcasted_iota(jnp.int32, s.shape, 0)
            qry = lax.broadcasted_iota(jnp.int32, s.shape, 1) & (tq - 1)
            s = jnp.where(key <= qry, s, NEG_BIG)
        m = m_ref[...]
        m_new = jnp.maximum(m, jnp.max(s, axis=0, keepdims=True))
        corr = jnp.exp2(m - m_new)
        p = jnp.exp2(s - m_new)
        m_ref[...] = m_new
        l_ref[...] = l_ref[...] * corr + jnp.sum(p, axis=0, keepdims=True)
        acc_ref[...] = acc_ref[...] * corr + jnp.dot(vb, p.astype(BF16), preferred_element_type=F32)

    m_ref[...] = jnp.full(m_ref.shape, NEG_BIG, F32)
    l_ref[...] = jnp.zeros(l_ref.shape, F32)
    acc_ref[...] = jnp.zeros(acc_ref.shape, F32)
    qk(0, 0)

    def pair(pi, carry):
        b0 = 2 * pi
        qk(b0 + 1, 1)
        upd(b0, 0, False)
        qk(b0 + 2, 0)
        upd(b0 + 1, 1, False)
        return carry

    lax.fori_loop(0, qi // 2, pair, 0)

    @pl.when(qi % 2 == 1)
    def _():
        qk(qi, 1)
        upd(qi - 1, 0, False)
        upd(qi, 1, True)

    @pl.when(qi % 2 == 0)
    def _():
        upd(qi, 0, True)

    l = l_ref[...]
    acc = acc_ref[...]
    lam = _lambda(lq1_ref, lk1_ref, lq2_ref, lk2_ref, lam_init)
    o = acc[:, :tq] / l[:, :tq] - lam * (acc[:, tq:] / l[:, tq:])
    y = o * lax.rsqrt(jnp.mean(o * o, axis=0, keepdims=True) + EPS) * gs_ref[...] * (1.0 - lam_init)
    o_ref[...] = y.T.astype(BF16)


def _flash_diff_attn(lams, g_sub_col, qt, k, vt, b, t, tq, lam_init):
    assert tq & (tq - 1) == 0 and t % tq == 0
    nq = t // tq
    n = b * t
    vec = lambda shape: pl.BlockSpec(shape, lambda bi, h, qi: (0, 0))
    return pl.pallas_call(
        functools.partial(_flash_kernel, tq=tq, lam_init=lam_init),
        grid=(b, A_HEADS, nq),
        in_specs=[vec((1, A_HD))] * 4 + [
            vec((LANES, 1)),
            pl.BlockSpec((1, LANES, tq), lambda bi, h, qi: (bi, h, qi)),
            pl.BlockSpec((t, LANES), lambda bi, h, qi: (bi, h)),
            pl.BlockSpec((1, LANES, t), lambda bi, h, qi: (bi, h, 0)),
        ],
        out_specs=pl.BlockSpec((tq, LANES), lambda bi, h, qi: (bi * nq + qi, h)),
        out_shape=jax.ShapeDtypeStruct((n, A_HEADS * LANES), BF16),
        scratch_shapes=[pltpu.VMEM((tq, 2 * tq), F32), pltpu.VMEM((tq, 2 * tq), F32),
                        pltpu.VMEM((1, 2 * tq), F32), pltpu.VMEM((1, 2 * tq), F32),
                        pltpu.VMEM((LANES, 2 * tq), F32)],
        compiler_params=_cparams(("arbitrary", "arbitrary", "arbitrary")),
        name="flash_diff_attn",
    )(*lams, g_sub_col, qt, k, vt)


PAGED_ROWS = 2 * A_HEADS
PAGED_WIDTH = A_HEADS * LANES


def _paged_start(q_ref, kn_ref, vn_ref, qd_ref, m_ref, l_ref, acc_ref):
    rows, width = PAGED_ROWS, PAGED_WIDTH
    q = jnp.broadcast_to(q_ref[0].astype(F32), (rows, width))
    row = lax.broadcasted_iota(jnp.int32, (rows, width), 0)
    lane = lax.broadcasted_iota(jnp.int32, (rows, width), 1)
    qd = jnp.where(lane // A_HD == row, q, 0.0)
    qd_ref[...] = qd.astype(BF16)
    s_self = jnp.sum(qd * kn_ref[0], axis=-1, keepdims=True)
    m_ref[...] = jnp.broadcast_to(s_self, (rows, LANES))
    l_ref[...] = jnp.ones((rows, LANES), F32)
    acc_ref[...] = jnp.broadcast_to(vn_ref[0], (rows, width))


def _paged_scores(k_refs, qd_ref):
    qd = qd_ref[...]
    return jnp.concatenate([jnp.dot(qd, k_ref[0].astype(BF16), preferred_element_type=F32)
                            for k_ref in k_refs], axis=1)


def _paged_softmax(s, m_ref, l_ref):
    rows = PAGED_ROWS
    m = m_ref[:, 0:1]
    m_new = jnp.maximum(m, jnp.max(s, axis=-1, keepdims=True))
    corr = jnp.exp2(m - m_new)
    p = jnp.exp2(s - m_new)
    l = l_ref[:, 0:1] * corr + jnp.sum(p, axis=-1, keepdims=True)
    m_ref[...] = jnp.broadcast_to(m_new, (rows, LANES))
    l_ref[...] = jnp.broadcast_to(l, (rows, LANES))
    return p.astype(BF16), corr


def _paged_values(v_refs, pb, heads):
    page = v_refs[0].shape[1] // A_HEADS
    outs = []
    for h in heads:
        vh = jnp.concatenate([v_ref[0, pl.ds(h, page, stride=A_HEADS), :].astype(BF16)
                              for v_ref in v_refs], axis=0)
        outs.append(jnp.dot(pb, vh, preferred_element_type=F32))
    return outs


def _paged_accumulate(acc_ref, corr, outs, heads):
    for h, o in zip(heads, outs):
        cols = slice(h * LANES, (h + 1) * LANES)
        acc_ref[:, cols] = acc_ref[:, cols] * corr + o


def _paged_pages(k_refs, v_refs, qd_ref, m_ref, l_ref, acc_ref):
    pb, corr = _paged_softmax(_paged_scores(k_refs, qd_ref), m_ref, l_ref)
    heads = range(A_HEADS)
    _paged_accumulate(acc_ref, corr, _paged_values(v_refs, pb, heads), heads)


def _paged_finish(lam_refs, gs_ref, l_ref, acc_ref, o_ref, lam_init):
    lam = _lambda(*lam_refs, lam_init)
    g = gs_ref[...]
    outs = []
    for h in range(A_HEADS):
        cols = slice(h * LANES, (h + 1) * LANES)
        a0 = acc_ref[2 * h:2 * h + 1, cols] / l_ref[2 * h:2 * h + 1, 0:1]
        a1 = acc_ref[2 * h + 1:2 * h + 2, cols] / l_ref[2 * h + 1:2 * h + 2, 0:1]
        outs.append(_subln(a0 - lam * a1, g, lam_init))
    o_ref[0] = jnp.concatenate(outs, axis=1).astype(BF16)


def _paged_kernel(pt_ref, lq1_ref, lk1_ref, lq2_ref, lk2_ref, gs_ref, q_ref, kn_ref, vn_ref, *rest,
                  pages, lam_init):
    k_refs = rest[:pages]
    v_refs = rest[pages:2 * pages]
    o_ref = rest[2 * pages]
    state = rest[2 * pages + 1:]
    j = pl.program_id(1)

    @pl.when(j == 0)
    def _():
        _paged_start(q_ref, kn_ref, vn_ref, *state)

    _paged_pages(k_refs, v_refs, *state)

    @pl.when(j == pl.num_programs(1) - 1)
    def _():
        _paged_finish((lq1_ref, lk1_ref, lq2_ref, lk2_ref), gs_ref, state[2], state[3], o_ref, lam_init)


def _paged_specs(pages, width, page, seq_of, group_of):
    vec = lambda shape: pl.BlockSpec(shape, lambda a, b, pt: (0, 0))
    row3 = pl.BlockSpec((1, 1, width), lambda a, b, pt: (seq_of(a, b), 0, 0))

    def page_spec(r, shape):
        return pl.BlockSpec((1,) + shape,
                            lambda a, b, pt: (pt[seq_of(a, b), group_of(a, b) * pages + r], 0, 0))

    in_specs = ([vec((1, A_HD))] * 4 + [vec((1, LANES)), row3, row3, row3]
                + [page_spec(r, (width, page)) for r in range(pages)]
                + [page_spec(r, (page * A_HEADS, LANES)) for r in range(pages)])
    scratch = [pltpu.VMEM((PAGED_ROWS, width), BF16), pltpu.VMEM((PAGED_ROWS, LANES), F32),
               pltpu.VMEM((PAGED_ROWS, LANES), F32), pltpu.VMEM((PAGED_ROWS, width), F32)]
    return in_specs, row3, scratch


def _paged_diff_attn(page_table, lams, g_sub, q, k_new, v_new, cache_kt, cache_v, pages, lam_init):
    db, n_pages = page_table.shape
    width, page = cache_kt.shape[1:]
    assert n_pages % pages == 0 and cache_v.shape[1:] == (page * A_HEADS, LANES) and width == PAGED_WIDTH
    in_specs, out_spec, scratch = _paged_specs(pages, width, page, lambda a, b: a, lambda a, b: b)
    grid_spec = pltpu.PrefetchScalarGridSpec(
        num_scalar_prefetch=1, grid=(db, n_pages // pages),
        in_specs=in_specs, out_specs=out_spec, scratch_shapes=scratch)
    return pl.pallas_call(
        functools.partial(_paged_kernel, pages=pages, lam_init=lam_init),
        grid_spec=grid_spec,
        out_shape=jax.ShapeDtypeStruct((db, 1, width), BF16),
        compiler_params=_cparams(("arbitrary", "arbitrary")),
        name="paged_diff_attn",
    )(page_table, *lams, g_sub, q, k_new, v_new, *([cache_kt] * pages), *([cache_v] * pages))


def _head_mask(x, e):
    lane = lax.broadcasted_iota(jnp.int32, x.shape, x.ndim - 1)
    keep = (lane >= B_DK) if e else (lane < B_DK)
    return jnp.where(keep, x, jnp.zeros_like(x))


def _rms_plain(o):
    return o * lax.rsqrt(jnp.mean(o * o, axis=-1, keepdims=True) + EPS)


def _retention_kernel(qk_ref, v_ref, gs_ref, din_ref, dq_ref, dk_ref, dc_ref, o_ref, s_out_ref, s_ref, *, c):
    ci = pl.program_id(1)

    @pl.when(ci == 0)
    def _():
        s_ref[...] = jnp.zeros_like(s_ref)

    half = B_HEADS * B_DK
    n_sub = qk_ref.shape[0] // c
    for p in range(B_HEADS // 2):
        state = s_ref[p]
        for sub in range(n_sub):
            tok = slice(sub * c, (sub + 1) * c)
            q2 = qk_ref[tok, p * LANES:(p + 1) * LANES]
            k2 = qk_ref[tok, half + p * LANES:half + (p + 1) * LANES]
            s_bf = state.astype(BF16)
            kd = (k2.astype(F32) * dk_ref[p]).astype(BF16)
            new_rows = []
            for e in range(2):
                h = 2 * p + e
                qz = _head_mask(q2, e)
                vh = v_ref[tok, h * LANES:(h + 1) * LANES]
                attn = lax.dot_general(qz, k2, NT_DIMS, preferred_element_type=F32) * din_ref[h]
                o = (jnp.dot(attn.astype(BF16), vh, preferred_element_type=F32)
                     + jnp.dot(qz, s_bf, preferred_element_type=F32) * dq_ref[h])
                u = lax.dot_general(kd, vh, TN_DIMS, preferred_element_type=F32)
                rows = slice(e * B_DK, (e + 1) * B_DK)
                new_rows.append(state[rows] * dc_ref[p][rows] + u[rows])
                gate = gs_ref[tok, h * LANES:(h + 1) * LANES].astype(F32)
                o_ref[tok, h * LANES:(h + 1) * LANES] = (_rms_plain(o) * gate).astype(BF16)
            state = jnp.concatenate(new_rows, axis=0)
        s_ref[p] = state

    @pl.when(ci == pl.num_programs(1) - 1)
    def _():
        s_out_ref[0] = s_ref[...].reshape(B_HEADS, B_DK, B_DV)


def _retention_prompt(qk, v, gs, b, t):
    c = math.gcd(t, RET_CHUNK)
    blk = math.gcd(t, RET_CHUNKS_PER_STEP * c)
    nc = t // blk
    n = b * t
    d_in, d_q, d_k, d_c = _retention_tables(c)
    rows = lambda bi, ci: (bi * nc + ci, 0)
    const3 = lambda bi, ci: (0, 0, 0)
    width = B_HEADS * B_DV
    return pl.pallas_call(
        functools.partial(_retention_kernel, c=c),
        grid=(b, nc),
        in_specs=[
            pl.BlockSpec((blk, 2 * B_HEADS * B_DK), rows),
            pl.BlockSpec((blk, width), rows),
            pl.BlockSpec((blk, width), rows),
            pl.BlockSpec(d_in.shape, const3),
            pl.BlockSpec(d_q.shape, const3),
            pl.BlockSpec(d_k.shape, const3),
            pl.BlockSpec(d_c.shape, const3),
        ],
        out_specs=[
            pl.BlockSpec((blk, width), rows),
            pl.BlockSpec((1, B_HEADS, B_DK, B_DV), lambda bi, ci: (bi, 0, 0, 0)),
        ],
        out_shape=[jax.ShapeDtypeStruct((n, width), BF16),
                   jax.ShapeDtypeStruct((b, B_HEADS, B_DK, B_DV), F32)],
        scratch_shapes=[pltpu.VMEM((B_HEADS // 2, 2 * B_DK, B_DV), F32)],
        compiler_params=_cparams(("arbitrary", "arbitrary")),
        name="retention_prompt",
    )(qk, v, gs, d_in, d_q, d_k, d_c)


def _retention_step_kernel(qk_ref, v_ref, gs_ref, s0_ref, dq_ref, dc_ref, o_ref, s_out_ref):
    half = B_HEADS * B_DK
    pad = BF16_SUBLANES
    row0 = lax.broadcasted_iota(jnp.int32, (pad, LANES), 0) == 0
    qk = qk_ref[0].astype(F32)
    v = v_ref[0].astype(F32)

    def pad_rows(x):
        return jnp.where(row0, jnp.broadcast_to(x, (pad, LANES)), 0.0).astype(BF16)

    for p in range(B_HEADS // 2):
        q2 = qk[:, p * LANES:(p + 1) * LANES]
        k2 = qk[:, half + p * LANES:half + (p + 1) * LANES]
        s_old = s0_ref[0, 2 * p:2 * p + 2].reshape(2 * B_DK, B_DV)
        s_bf = s_old.astype(BF16)
        k8 = pad_rows(k2)
        new_rows = []
        for e in range(2):
            h = 2 * p + e
            qz = _head_mask(q2, e)
            vh = v[:, h * LANES:(h + 1) * LANES]
            qk_dot = jnp.sum(qz * k2, axis=-1, keepdims=True)
            qs = jnp.dot(pad_rows(qz), s_bf, preferred_element_type=F32)[0:1]
            o = qk_dot * vh + qs * dq_ref[h][0:1]
            u = lax.dot_general(k8, pad_rows(vh), TN_DIMS, preferred_element_type=F32)
            rows = slice(e * B_DK, (e + 1) * B_DK)
            new_rows.append(s_old[rows] * dc_ref[p][rows] + u[rows])
            gate = gs_ref[0][:, h * LANES:(h + 1) * LANES].astype(F32)
            o_ref[0, :, h * LANES:(h + 1) * LANES] = (_rms_plain(o) * gate).astype(BF16)
        s_out_ref[0, 2 * p:2 * p + 2] = jnp.concatenate(new_rows, axis=0).reshape(2, B_DK, B_DV)


def _retention_sample(qk, v, gs, s0):
    db = qk.shape[0]
    _, d_q, _, d_c = _retention_tables(1)
    width = B_HEADS * B_DV
    row3 = lambda w: pl.BlockSpec((1, 1, w), lambda bi: (bi, 0, 0))
    state = pl.BlockSpec((1, B_HEADS, B_DK, B_DV), lambda bi: (bi, 0, 0, 0))
    const3 = lambda bi: (0, 0, 0)
    return pl.pallas_call(
        _retention_step_kernel,
        grid=(db,),
        in_specs=[row3(2 * B_HEADS * B_DK), row3(width), row3(width), state,
                  pl.BlockSpec(d_q.shape, const3), pl.BlockSpec(d_c.shape, const3)],
        out_specs=[row3(width), state],
        out_shape=[jax.ShapeDtypeStruct((db, 1, width), BF16),
                   jax.ShapeDtypeStruct(s0.shape, F32)],
        compiler_params=_cparams(("arbitrary",)),
        name="retention_sample",
    )(qk, v, gs, s0, d_q, d_c)


def _rms_matmul_kernel(x_ref, g_ref, w_ref, of_ref, ob_ref, xn_ref):
    @pl.when(pl.program_id(1) == 0)
    def _():
        x = x_ref[...]
        xn = x * lax.rsqrt(jnp.mean(x * x, axis=-1, keepdims=True) + EPS) * g_ref[...]
        xn_ref[...] = xn.astype(BF16)

    r = jnp.dot(xn_ref[...], w_ref[...], preferred_element_type=F32)
    of_ref[...] = r
    ob_ref[...] = r.astype(BF16)


def _rms_matmul(x, g, w, tm, tn):
    n, d = x.shape
    width = w.shape[1]
    return pl.pallas_call(
        _rms_matmul_kernel,
        grid=(n // tm, width // tn),
        in_specs=[pl.BlockSpec((tm, d), lambda i, j: (i, 0)),
                  pl.BlockSpec((1, d), lambda i, j: (0, 0)),
                  pl.BlockSpec((d, tn), lambda i, j: (0, j))],
        out_specs=[pl.BlockSpec((tm, tn), lambda i, j: (i, j))] * 2,
        out_shape=[jax.ShapeDtypeStruct((n, width), F32), jax.ShapeDtypeStruct((n, width), BF16)],
        scratch_shapes=[pltpu.VMEM((tm, d), BF16)],
        compiler_params=_cparams(("arbitrary", "arbitrary")),
        name="mem_kv",
    )(x, g, w)


def _softmax_rows(s):
    m = jnp.max(s, axis=-1, keepdims=True)
    p = jnp.exp(s - m)
    return p / jnp.sum(p, axis=-1, keepdims=True)


def _mem_attn_kernel(q_ref, mk_ref, mv_ref, o_ref):
    for h in range(C_HEADS):
        cols = slice(h * C_HD, (h + 1) * C_HD)
        s = lax.dot_general(q_ref[:, cols], mk_ref[:, cols], NT_DIMS, preferred_element_type=F32)
        p = _softmax_rows(s)
        o_ref[:, cols] = jnp.dot(p.astype(BF16), mv_ref[:, cols], preferred_element_type=F32).astype(BF16)


def _mem_attn_prompt(q, mk, mv, t, tm):
    n, width = q.shape
    mem_len = mk.shape[0] // (n // t)
    per_seq = t // tm
    return pl.pallas_call(
        _mem_attn_kernel,
        grid=(n // tm,),
        in_specs=[pl.BlockSpec((tm, width), lambda i: (i, 0)),
                  pl.BlockSpec((mem_len, width), lambda i: (i // per_seq, 0)),
                  pl.BlockSpec((mem_len, width), lambda i: (i // per_seq, 0))],
        out_specs=pl.BlockSpec((tm, width), lambda i: (i, 0)),
        out_shape=jax.ShapeDtypeStruct((n, width), BF16),
        compiler_params=_cparams(("arbitrary",)),
        name="mem_attn_prompt",
    )(q, mk, mv)


def _mem_attn_step_kernel(q_ref, mk_ref, mv_ref, o_ref):
    pad = BF16_SUBLANES
    width = C_HEADS * C_HD
    q = jnp.broadcast_to(q_ref[0].astype(F32), (pad, width))
    row = lax.broadcasted_iota(jnp.int32, (pad, width), 0)
    lane = lax.broadcasted_iota(jnp.int32, (pad, width), 1)
    qd = jnp.where(lane // C_HD == row, q, 0.0).astype(BF16)
    s = lax.dot_general(qd, mk_ref[0].astype(BF16), NT_DIMS, preferred_element_type=F32)
    p = _softmax_rows(s)
    o = jnp.dot(p.astype(BF16), mv_ref[0].astype(BF16), preferred_element_type=F32)
    o_ref[0] = jnp.concatenate([o[h:h + 1, h * C_HD:(h + 1) * C_HD] for h in range(C_HEADS)],
                               axis=1).astype(BF16)


def _mem_attn_sample(q, mk, mv):
    db, mem_len, width = mk.shape
    row3 = pl.BlockSpec((1, 1, width), lambda bi: (bi, 0, 0))
    mem = pl.BlockSpec((1, mem_len, width), lambda bi: (bi, 0, 0))
    return pl.pallas_call(
        _mem_attn_step_kernel,
        grid=(db,),
        in_specs=[row3, mem, mem],
        out_specs=row3,
        out_shape=jax.ShapeDtypeStruct((db, 1, width), BF16),
        compiler_params=_cparams(("arbitrary",)),
        name="mem_attn_sample",
    )(q, mk, mv)


def _merge_kernel(x_ref, oa_ref, ob_ref, oc_ref, gates_ref, pa_ref, pb_ref, pc_ref, wo_ref, h_ref):
    d = x_ref.shape[1]
    gate = lambda k: _sigmoid(gates_ref[:, k * d:(k + 1) * d].astype(F32))
    merged = (gate(0) * jnp.dot(oa_ref[...], pa_ref[...], preferred_element_type=F32)
              + gate(1) * jnp.dot(ob_ref[...], pb_ref[...], preferred_element_type=F32)
              + gate(2) * jnp.dot(oc_ref[...], pc_ref[...], preferred_element_type=F32))
    h_ref[...] = x_ref[...] + jnp.dot(merged.astype(BF16), wo_ref[...], preferred_element_type=F32)


def _merge(x, oa, ob, oc, gates, p_a, p_b, p_c, w_o, tm):
    n, d = x.shape
    rows = lambda w: pl.BlockSpec((tm, w), lambda i: (i, 0))
    whole = lambda a: _resident(a.shape, lambda i: (0, 0))
    return pl.pallas_call(
        _merge_kernel,
        grid=(n // tm,),
        in_specs=[rows(d), rows(oa.shape[1]), rows(ob.shape[1]), rows(oc.shape[1]), rows(gates.shape[1]),
                  whole(p_a), whole(p_b), whole(p_c), whole(w_o)],
        out_specs=rows(d),
        out_shape=jax.ShapeDtypeStruct((n, d), F32),
        compiler_params=_cparams(("arbitrary",)),
        name="merge",
    )(x, oa, ob, oc, gates, p_a, p_b, p_c, w_o)


def _mlp_kernel(h_ref, g_ref, wu_ref, wd_ref, gf_ref, y_ref, hn_ref, acc_ref):
    f = pl.program_id(1)

    @pl.when(f == 0)
    def _():
        h = h_ref[...]
        hn = h * lax.rsqrt(jnp.mean(h * h, axis=-1, keepdims=True) + EPS) * g_ref[...]
        hn_ref[...] = hn.astype(BF16)
        acc_ref[...] = jnp.zeros_like(acc_ref)

    u = jnp.dot(hn_ref[...], wu_ref[...], preferred_element_type=F32)
    a = jnp.square(jnp.maximum(u, 0.0)).astype(BF16)
    acc_ref[...] += jnp.dot(a, wd_ref[...], preferred_element_type=F32)

    @pl.when(f == pl.num_programs(1) - 1)
    def _():
        h2 = h_ref[...] + acc_ref[...]
        y_ref[...] = h2 * lax.rsqrt(jnp.mean(h2 * h2, axis=-1, keepdims=True) + EPS) * gf_ref[...]


def _mlp_paged_kernel(pt_ref, h_ref, g_ref, wu_ref, wd_ref, gf_ref, lq1_ref, lk1_ref, lq2_ref, lk2_ref,
                      gs_ref, q_ref, kn_ref, vn_ref, *rest, pages, lam_init):
    k_refs = rest[:pages]
    v_refs = rest[pages:2 * pages]
    y_ref, o_ref, hn_ref, acc_ref = rest[2 * pages:2 * pages + 4]
    state = rest[2 * pages + 4:]
    f = pl.program_id(1)
    last = pl.num_programs(1) - 1

    @pl.when(f == 0)
    def _():
        h = h_ref[...]
        hn = h * lax.rsqrt(jnp.mean(h * h, axis=-1, keepdims=True) + EPS) * g_ref[...]
        hn_ref[...] = hn.astype(BF16)
        acc_ref[...] = jnp.zeros_like(acc_ref)
        _paged_start(q_ref, kn_ref, vn_ref, *state)

    qd_ref, pm_ref, pl_ref, pacc_ref = state
    s = _paged_scores(k_refs, qd_ref)
    u = jnp.dot(hn_ref[...], wu_ref[...], preferred_element_type=F32)
    pb, corr = _paged_softmax(s, pm_ref, pl_ref)
    first, second = range(A_HEADS // 2), range(A_HEADS // 2, A_HEADS)
    o_first = _paged_values(v_refs, pb, first)
    a = jnp.square(jnp.maximum(u, 0.0)).astype(BF16)
    o_second = _paged_values(v_refs, pb, second)
    acc_ref[...] += jnp.dot(a, wd_ref[...], preferred_element_type=F32)
    _paged_accumulate(pacc_ref, corr, o_first, first)
    _paged_accumulate(pacc_ref, corr, o_second, second)

    @pl.when(f == last)
    def _():
        h2 = h_ref[...] + acc_ref[...]
        y_ref[...] = h2 * lax.rsqrt(jnp.mean(h2 * h2, axis=-1, keepdims=True) + EPS) * gf_ref[...]
        _paged_finish((lq1_ref, lk1_ref, lq2_ref, lk2_ref), gs_ref, state[2], state[3], o_ref, lam_init)


def _mlp_with_paged(h, g_mlp, w_up, w_down, g_final, tm, tf,
                    page_table, lams, g_sub, q, k_new, v_new, cache_kt, cache_v, lam_init):
    n, d = h.shape
    d_ff = w_up.shape[1]
    db, n_pages = page_table.shape
    n_f = d_ff // tf
    width, page = cache_kt.shape[1:]
    assert n // tm == db and n_pages % n_f == 0
    pages = n_pages // n_f
    paged_in, paged_out, paged_scratch = _paged_specs(pages, width, page, lambda i, f: i, lambda i, f: f)
    grid_spec = pltpu.PrefetchScalarGridSpec(
        num_scalar_prefetch=1,
        grid=(n // tm, n_f),
        in_specs=[pl.BlockSpec((tm, d), lambda i, f, pt: (i, 0)),
                  pl.BlockSpec((1, d), lambda i, f, pt: (0, 0)),
                  pl.BlockSpec((d, tf), lambda i, f, pt: (0, f)),
                  pl.BlockSpec((tf, d), lambda i, f, pt: (f, 0)),
                  pl.BlockSpec((1, d), lambda i, f, pt: (0, 0))] + paged_in,
        out_specs=[pl.BlockSpec((tm, d), lambda i, f, pt: (i, 0)), paged_out],
        scratch_shapes=[pltpu.VMEM((tm, d), BF16), pltpu.VMEM((tm, d), F32)] + paged_scratch,
    )
    return pl.pallas_call(
        functools.partial(_mlp_paged_kernel, pages=pages, lam_init=lam_init),
        grid_spec=grid_spec,
        out_shape=[jax.ShapeDtypeStruct((n, d), F32), jax.ShapeDtypeStruct((db, 1, width), BF16)],
        compiler_params=_cparams(("arbitrary", "arbitrary")),
        name="mlp_paged",
    )(page_table, h, g_mlp, w_up, w_down, g_final, *lams, g_sub, q, k_new, v_new,
      *([cache_kt] * pages), *([cache_v] * pages))


def _mlp(h, g_mlp, w_up, w_down, g_final, tm, tf):
    n, d = h.shape
    d_ff = w_up.shape[1]
    return pl.pallas_call(
        _mlp_kernel,
        grid=(n // tm, d_ff // tf),
        in_specs=[pl.BlockSpec((tm, d), lambda i, f: (i, 0)),
                  pl.BlockSpec((1, d), lambda i, f: (0, 0)),
                  pl.BlockSpec((d, tf), lambda i, f: (0, f)),
                  pl.BlockSpec((tf, d), lambda i, f: (f, 0)),
                  pl.BlockSpec((1, d), lambda i, f: (0, 0))],
        out_specs=pl.BlockSpec((tm, d), lambda i, f: (i, 0)),
        out_shape=jax.ShapeDtypeStruct((n, d), F32),
        scratch_shapes=[pltpu.VMEM((tm, d), BF16), pltpu.VMEM((tm, d), F32)],
        compiler_params=_cparams(("arbitrary", "arbitrary")),
        name="mlp",
    )(h, g_mlp, w_up, w_down, g_final)


class _Tiles(NamedTuple):
    rows: int
    merge_rows: int
    flash: int
    memkv_rows: int
    memkv_cols: int
    mlp_ff: int
    mlp_ff_fused: int
    pages: int


def _tiles(t, n_mem):
    return _Tiles(rows=min(t, 512), merge_rows=min(t, 256), flash=min(t, 512),
                  memkv_rows=min(n_mem, 256), memkv_cols=1024, mlp_ff=1024, mlp_ff_fused=512, pages=8)


def kernel(x_prompt, x_sample, cache_diff_k, cache_diff_v, state_ret, cache_mem_k, cache_mem_v, page_table, mem_prompt, g_mix, w_in, lam_q1, lam_k1, lam_q2, lam_k2, g_subln, g_mem, w_mem_k, w_mem_v, p_a, p_b, p_c, w_o, g_mlp, w_up, w_down, g_final):
    depth = g_mix.shape[0]
    assert depth == 1
    layer = 0
    lam_init = 0.8 - 0.6 * math.exp(-0.3 * layer)
    b, t, d = x_prompt.shape
    db, dt, _ = x_sample.shape
    assert dt == 1
    n_pool, page = cache_diff_k.shape[1:3]
    past_len = page_table.shape[1] * page
    mem_len = mem_prompt.shape[1]
    n_p, n_s = b * t, db * dt

    row = lambda a: a[layer].reshape(1, -1)
    lams = (row(lam_q1), row(lam_k1), row(lam_q2), row(lam_k2))
    g_sub = row(g_subln)
    w_in_b = w_in[layer].astype(BF16)
    w_mem_b = jnp.concatenate([w_mem_k[layer], w_mem_v[layer]], axis=1).astype(BF16)
    pa_b, pb_b, pc_b, wo_b = (a[layer].astype(BF16) for a in (p_a, p_b, p_c, w_o))
    wu_b, wd_b = w_up[layer].astype(BF16), w_down[layer].astype(BF16)
    g_final2 = g_final.reshape(1, -1)

    tiles = _tiles(t, b * mem_len)
    tm_p = tiles.rows
    xp = x_prompt.reshape(n_p, d)
    tab_p = _rope_lane_tables(jnp.arange(t))
    cos_a, sin_a = _rope_angles(jnp.arange(t), A_ROT, ROPE_THETA)
    tab_t = jnp.concatenate([cos_a.T, sin_a.T], axis=0)
    qt, ktf, kab, vaf, vtb, qkb, vb, gsb, qc, gates = _inproj(xp, row(g_mix), w_in_b, tab_p, tm_p,
                                                              seq_len=t, tab_t=tab_t)
    oa = _flash_diff_attn(lams, g_sub.reshape(-1, 1), qt, kab, vtb, b, t, tiles.flash, lam_init)
    ob, state_p = _retention_prompt(qkb, vb, gsb, b, t)
    mem_f, mem_b = _rms_matmul(mem_prompt.reshape(b * mem_len, d), row(g_mem), w_mem_b,
                               tiles.memkv_rows, tiles.memkv_cols)
    c_w = C_HEADS * C_HD
    oc = _mem_attn_prompt(qc, mem_b[:, :c_w], mem_b[:, c_w:], t, tm_p)
    hp = _merge(xp, oa, ob, oc, gates, pa_b, pb_b, pc_b, wo_b, tiles.merge_rows)

    xs = x_sample.reshape(n_s, d)
    tab_s = jnp.tile(_rope_lane_tables(past_len + jnp.arange(dt)), (db, 1))
    qa_s, kaf_s, vaf_s, qkb_s, vb_s, gsb_s, qc_s, gates_s = _inproj(xs, row(g_mix), w_in_b, tab_s, n_s)
    width_a = A_HEADS * LANES
    r3 = lambda a: a.reshape(db, 1, a.shape[-1])
    ckt = cache_diff_k[layer].transpose(0, 2, 3, 4, 1).reshape(n_pool, width_a, page)
    cv = cache_diff_v[layer].reshape(n_pool, page * A_HEADS, LANES)
    paged_args = (page_table, lams, g_sub, r3(qa_s), r3(kaf_s), r3(vaf_s), ckt, cv)
    d_ff = wu_b.shape[1]
    if n_p // tm_p == db and page_table.shape[1] % (d_ff // tiles.mlp_ff_fused) == 0:
        y_p, oa_s = _mlp_with_paged(hp, row(g_mlp), wu_b, wd_b, g_final2, tm_p, tiles.mlp_ff_fused,
                                    *paged_args, lam_init)
    else:
        y_p = _mlp(hp, row(g_mlp), wu_b, wd_b, g_final2, tm_p, tiles.mlp_ff)
        oa_s = _paged_diff_attn(*paged_args, tiles.pages, lam_init)
    ob_s, state_s = _retention_sample(r3(qkb_s), r3(vb_s), r3(gsb_s), state_ret[layer])
    oc_s = _mem_attn_sample(r3(qc_s), cache_mem_k[layer].reshape(db, mem_len, c_w),
                            cache_mem_v[layer].reshape(db, mem_len, c_w))
    hs = _merge(xs, oa_s.reshape(n_s, -1), ob_s.reshape(n_s, -1), oc_s.reshape(n_s, -1), gates_s,
                pa_b, pb_b, pc_b, wo_b, n_s)
    y_s = _mlp(hs, row(g_mlp), wu_b, wd_b, g_final2, n_s, tiles.mlp_ff)

    return (
        y_p.reshape(b, t, d),
        y_s.reshape(db, dt, d),
        ktf.reshape(1, b, A_HEADS, 2, A_HD, t).transpose(0, 1, 5, 2, 3, 4),
        vaf.reshape(1, b, t, A_HEADS, 2 * A_HD),
        state_p[None],
        mem_f[:, :c_w].reshape(1, b, mem_len, C_HEADS, C_HD),
        mem_f[:, c_w:].reshape(1, b, mem_len, C_HEADS, C_HD),
        kaf_s.reshape(1, db, dt, A_HEADS, 2, A_HD),
        vaf_s.reshape(1, db, dt, A_HEADS, 2 * A_HD),
        state_s[None],
    )
```
